```python
import jax
import jax.numpy as jnp
from jax import lax
import numpy as np

D_MODEL = 1024
BATCH = 2
SEQ = 8192
DEPTH = 1
DEC_BATCH = 32
DEC_SEQ = 8
PAST_LEN = 8192
PAGE_SIZE = 128

N_META = 16
HEAD_DIM = 64
N_HEADS_A = 8
N_HEADS_B = 8
N_IDX_HEADS = 8
IDX_DIM = 64
TOP_K_MAX = 256
Q_BLOCK = 128
ROPE_THETA = 10000.0
D_FF = 2816
CONV_W = 3
RMS_EPS = 1e-6
WIDTH_A = N_HEADS_A * HEAD_DIM
WIDTH_B = N_HEADS_B * HEAD_DIM
SPLIT_SIZES = (WIDTH_A, WIDTH_A, WIDTH_A, N_IDX_HEADS * IDX_DIM, IDX_DIM, N_IDX_HEADS,
               WIDTH_B, WIDTH_B, WIDTH_B, D_MODEL, D_MODEL)
IN_WIDTH = sum(SPLIT_SIZES)

kernel_name = 'hybrid_dsa_stickbreaking_convffn_step'


def rms_norm(x, g):
    x32 = x.astype(jnp.float32)
    y = x32 * lax.rsqrt(jnp.mean(x32 * x32, axis=-1, keepdims=True) + RMS_EPS)
    return (y * g.astype(jnp.float32)).astype(x.dtype)


def rope(x, pos):
    d = x.shape[-1]
    half = d // 2
    inv = ROPE_THETA ** (-jnp.arange(half, dtype=jnp.float32) * 2.0 / d)
    ang = pos.astype(jnp.float32)[:, None] * inv[None, :]
    cos = jnp.cos(ang)[:, None, :]
    sin = jnp.sin(ang)[:, None, :]
    x32 = x.astype(jnp.float32)
    x1, x2 = x32[..., :half], x32[..., half:]
    return jnp.concatenate([x1 * cos - x2 * sin, x1 * sin + x2 * cos], axis=-1).astype(x.dtype)


def in_projection(h, pos, w_in):
    b, s = h.shape[0], h.shape[1]
    offsets = np.cumsum(SPLIT_SIZES)[:-1].tolist()
    q_a, k_a, v_a, q_i, k_i, w_i, q_b, k_b, v_b, g_a, g_b = jnp.split(h @ w_in, offsets, axis=-1)
    heads = lambda t, n: t.reshape(b, s, n, -1)
    q_a = rope(heads(q_a, N_HEADS_A), pos)
    k_a = rope(heads(k_a, N_HEADS_A), pos)
    v_a = heads(v_a, N_HEADS_A)
    q_i = rope(heads(q_i, N_IDX_HEADS), pos)
    k_i = rope(k_i[:, :, None, :], pos)[:, :, 0, :]
    w_i = w_i * (N_IDX_HEADS ** -0.5)
    q_b = heads(q_b, N_HEADS_B)
    k_b = heads(k_b, N_HEADS_B)
    v_b = heads(v_b, N_HEADS_B)
    return q_a, k_a, v_a, q_i, k_i, w_i, q_b, k_b, v_b, g_a, g_b


def dsa_attend(q, q_i, w_i, q_pos, k, v, k_i, k_pos, top_k):
    rel = jax.nn.relu(jnp.einsum('bqhd,bld->bqhl', q_i, k_i).astype(jnp.float32) * (IDX_DIM ** -0.5))
    score = jnp.einsum('bqh,bqhl->bql', w_i.astype(jnp.float32), rel)
    causal = k_pos[None, :] <= q_pos[:, None]
    score = jnp.where(causal[None], score, -jnp.inf)
    _, idx = lax.top_k(score, top_k)
    valid = k_pos[idx] <= q_pos[None, :, None]
    take = jax.vmap(lambda rows, ii: rows[ii])
    k_sel = take(k, idx)
    v_sel = take(v, idx)
    logits = jnp.einsum('bqhd,bqkhd->bhqk', q, k_sel).astype(jnp.float32) * (HEAD_DIM ** -0.5)
    logits = jnp.where(valid[:, None], logits, -jnp.inf)
    p = jax.nn.softmax(logits, axis=-1)
    return jnp.einsum('bhqk,bqkhd->bqhd', p.astype(v.dtype), v_sel)


def stick_breaking_attend(q, q_pos, k, v, k_pos):
    z = jnp.einsum('bqhd,blhd->bhql', q, k).astype(jnp.float32) * (HEAD_DIM ** -0.5)
    strict = (k_pos[None, :] < q_pos[:, None])[None, None]
    log_keep = jnp.where(strict, jax.nn.log_sigmoid(-z), 0.0)
    after = lax.cumsum(log_keep, axis=3, reverse=True) - log_keep
    log_a = jnp.where(strict, jax.nn.log_sigmoid(z) + after, -jnp.inf)
    return jnp.einsum('bhql,blhd->bqhd', jnp.exp(log_a).astype(v.dtype), v)


def sweep_query_blocks(fn, qs, q_pos):
    n_blk = (q_pos.shape[0] - N_META) // Q_BLOCK
    head = fn(*[q[:, :N_META] for q in qs], q_pos[:N_META])

    def to_blocks(a):
        r = a[:, N_META:]
        r = r.reshape((r.shape[0], n_blk, Q_BLOCK) + r.shape[2:])
        return jnp.moveaxis(r, 1, 0)

    blocks = tuple(to_blocks(q) for q in qs) + (q_pos[N_META:].reshape(n_blk, Q_BLOCK),)
    tail = lax.map(lambda args: fn(*args), blocks)
    tail = jnp.moveaxis(tail, 0, 1)
    tail = tail.reshape((tail.shape[0], n_blk * Q_BLOCK) + tail.shape[3:])
    return jnp.concatenate([head, tail], axis=1)


def merge_branches(o_a, o_b, g_a, g_b, w_branch_a, w_branch_b, w_o):
    b, s = o_a.shape[0], o_a.shape[1]
    p_a = o_a.reshape(b, s, WIDTH_A) @ w_branch_a
    p_b = o_b.reshape(b, s, WIDTH_B) @ w_branch_b
    m = jax.nn.sigmoid(g_a) * p_a + jax.nn.sigmoid(g_b) * p_b
    return m @ w_o


def conv_ffn(h, conv_prev, w_up, conv_w, conv_b, w_down):
    s = h.shape[1]
    a, b = jnp.split(h @ w_up, 2, axis=-1)
    a_ext = jnp.concatenate([conv_prev.astype(a.dtype), a], axis=1)
    a_c = conv_b + conv_w[0] * a_ext[:, 0:s]
    for j in range(1, CONV_W):
        a_c = a_c + conv_w[j] * a_ext[:, j:j + s]
    out = (jax.nn.gelu(a_c, approximate=False) * b) @ w_down
    return out, a_ext[:, s:]


def gather_pages(cache, layer, page_table):
    c = cache[layer, page_table]
    return c.reshape((c.shape[0], c.shape[1] * c.shape[2]) + c.shape[3:])


def setup_inputs(seed: int = 0) -> dict:
    key = jax.random.key(seed)
    ks = jax.random.split(key, 24)
    n_pages = PAST_LEN // PAGE_SIZE
    n_phys = (DEC_BATCH * n_pages * 5) // 4
    nrm = lambda k, shape, scale: jax.random.normal(k, shape, jnp.float32) * scale
    page_table = jax.random.permutation(ks[8], n_phys)[:DEC_BATCH * n_pages]
    page_table = page_table.reshape(DEC_BATCH, n_pages).astype(jnp.int32)
    return {
        'x_prompt': nrm(ks[0], (BATCH, SEQ, D_MODEL), 1.0),
        'x_sample': nrm(ks[1], (DEC_BATCH, DEC_SEQ, D_MODEL), 1.0),
        'cache_k_a': nrm(ks[2], (DEPTH, n_phys, PAGE_SIZE, N_HEADS_A, HEAD_DIM), 1.0),
        'cache_v_a': nrm(ks[3], (DEPTH, n_phys, PAGE_SIZE, N_HEADS_A, HEAD_DIM), 1.0),
        'cache_k_idx': nrm(ks[4], (DEPTH, n_phys, PAGE_SIZE, IDX_DIM), 1.0),
        'cache_k_b': nrm(ks[5], (DEPTH, n_phys, PAGE_SIZE, N_HEADS_B, HEAD_DIM), 1.0),
        'cache_v_b': nrm(ks[6], (DEPTH, n_phys, PAGE_SIZE, N_HEADS_B, HEAD_DIM), 1.0),
        'state_conv': nrm(ks[7], (DEPTH, DEC_BATCH, CONV_W - 1, D_FF), 1.0),
        'page_table': page_table,
        'meta_tokens': nrm(ks[9], (N_META, D_MODEL), 1.0),
        'g_attn': 1.0 + nrm(ks[10], (DEPTH, D_MODEL), 0.02),
        'w_in': nrm(ks[11], (DEPTH, D_MODEL, IN_WIDTH), D_MODEL ** -0.5),
        'w_branch_a': nrm(ks[12], (DEPTH, WIDTH_A, D_MODEL), WIDTH_A ** -0.5),
        'w_branch_b': nrm(ks[13], (DEPTH, WIDTH_B, D_MODEL), WIDTH_B ** -0.5),
        'w_o': nrm(ks[14], (DEPTH, D_MODEL, D_MODEL), D_MODEL ** -0.5),
        'g_ffn': 1.0 + nrm(ks[15], (DEPTH, D_MODEL), 0.02),
        'w_up': nrm(ks[16], (DEPTH, D_MODEL, 2 * D_FF), D_MODEL ** -0.5),
        'conv_w': nrm(ks[17], (DEPTH, CONV_W, D_FF), CONV_W ** -0.5),
        'conv_b': nrm(ks[18], (DEPTH, D_FF), 0.02),
        'w_down': nrm(ks[19], (DEPTH, D_FF, D_MODEL), D_FF ** -0.5),
        'g_final': 1.0 + nrm(ks[20], (D_MODEL,), 0.02),
    }


def reference(x_prompt, x_sample, cache_k_a, cache_v_a, cache_k_idx, cache_k_b, cache_v_b, state_conv,
              page_table, meta_tokens, g_attn, w_in, w_branch_a, w_branch_b, w_o, g_ffn, w_up, conv_w,
              conv_b, w_down, g_final):
    batch, seq = x_prompt.shape[0], x_prompt.shape[1]
    dec_seq = x_sample.shape[1]
    past_len = page_table.shape[1] * PAGE_SIZE
    top_k_p = min(TOP_K_MAX, seq // 4)
    top_k_s = min(TOP_K_MAX, (past_len + dec_seq) // 4)
    pos_p = jnp.arange(N_META + seq, dtype=jnp.int32)
    pos_s = past_len + jnp.arange(dec_seq, dtype=jnp.int32)
    kpos_s = jnp.arange(past_len + dec_seq, dtype=jnp.int32)
    meta = jnp.broadcast_to(meta_tokens.astype(x_prompt.dtype)[None], (batch, N_META, D_MODEL))
    xp = jnp.concatenate([meta, x_prompt], axis=1)
    xs = x_sample
    new_p = [[] for _ in range(6)]
    new_s = [[] for _ in range(6)]
    for l in range(DEPTH):
        qa, ka, va, qi, ki, wi, qb, kb, vb, ga, gb = in_projection(rms_norm(xp, g_attn[l]), pos_p, w_in[l])
        oa = sweep_query_blocks(
            lambda q, q_i, w_i, qp: dsa_attend(q, q_i, w_i, qp, ka, va, ki, pos_p, top_k_p), (qa, qi, wi), pos_p)
        ob = sweep_query_blocks(lambda q, qp: stick_breaking_attend(q, qp, kb, vb, pos_p), (qb,), pos_p)
        xp = xp + merge_branches(oa, ob, ga, gb, w_branch_a[l], w_branch_b[l], w_o[l])
        f_p, conv_p = conv_ffn(rms_norm(xp, g_ffn[l]), jnp.zeros((batch, CONV_W - 1, D_FF), xp.dtype),
                               w_up[l], conv_w[l], conv_b[l], w_down[l])
        xp = xp + f_p
        qa_s, ka_s, va_s, qi_s, ki_s, wi_s, qb_s, kb_s, vb_s, ga_s, gb_s = in_projection(
            rms_norm(xs, g_attn[l]), pos_s, w_in[l])
        ka_all = jnp.concatenate([gather_pages(cache_k_a, l, page_table).astype(ka_s.dtype), ka_s], axis=1)
        va_all = jnp.concatenate([gather_pages(cache_v_a, l, page_table).astype(va_s.dtype), va_s], axis=1)
        ki_all = jnp.concatenate([gather_pages(cache_k_idx, l, page_table).astype(ki_s.dtype), ki_s], axis=1)
        kb_all = jnp.concatenate([gather_pages(cache_k_b, l, page_table).astype(kb_s.dtype), kb_s], axis=1)
        vb_all = jnp.concatenate([gather_pages(cache_v_b, l, page_table).astype(vb_s.dtype), vb_s], axis=1)
        oa_s = dsa_attend(qa_s, qi_s, wi_s, pos_s, ka_all, va_all, ki_all, kpos_s, top_k_s)
        ob_s = stick_breaking_attend(qb_s, pos_s, kb_all, vb_all, kpos_s)
        xs = xs + merge_branches(oa_s, ob_s, ga_s, gb_s, w_branch_a[l], w_branch_b[l], w_o[l])
        f_s, conv_s = conv_ffn(rms_norm(xs, g_ffn[l]), state_conv[l], w_up[l], conv_w[l], conv_b[l], w_down[l])
        xs = xs + f_s
        for lst, arr in zip(new_p, (ka, va, ki, kb, vb, conv_p)):
            lst.append(arr)
        for lst, arr in zip(new_s, (ka_s, va_s, ki_s, kb_s, vb_s, conv_s)):
            lst.append(arr)
    y_prompt = rms_norm(xp[:, N_META:], g_final)
    y_sample = rms_norm(xs, g_final)
    return (y_prompt, y_sample,
            jnp.stack(new_p[0]), jnp.stack(new_p[1]), jnp.stack(new_p[2]),
            jnp.stack(new_p[3]), jnp.stack(new_p[4]), jnp.stack(new_p[5]),
            jnp.stack(new_s[0]), jnp.stack(new_s[1]), jnp.stack(new_s[2]),
            jnp.stack(new_s[3]), jnp.stack(new_s[4]), jnp.stack(new_s[5]))
```

```python
import functools

import numpy as np
import jax
import jax.numpy as jnp
from jax import lax
from jax.experimental import pallas as pl
from jax.experimental.pallas import tpu as pltpu

F32 = jnp.float32
BF16 = jnp.bfloat16
I32 = jnp.int32

D_MODEL = 1024
N_HEADS = 8
HEAD_DIM = 64
WIDTH = N_HEADS * HEAD_DIM
IDX_DIM = 64
N_META = 16
PAGE_SIZE = 128
TOP_K_MAX = 256
ROPE_THETA = 10000.0
D_FF = 2816
RMS_EPS = 1e-6

LANES = 128
SUBLANES = 8
TQ = 256
TK = 256
TR = 256
RUN = TK // SUBLANES
VMEM_LIMIT = 56 * 1024 * 1024

INT_MIN = -2 ** 31
NEG_BIG = -1e30

C_GA, C_GB = 0, 1024
C_QA, C_KA, C_QI = 2048, 2560, 3072
C_KI, C_WI = 3584, 3648
C_VA, C_QB, C_KB, C_VB = 3712, 4224, 4736, 5248
NW = 5760
ROPE_LO, ROPE_HI = C_QA, C_VA


def _cparams(sem):
    return pltpu.CompilerParams(dimension_semantics=sem, vmem_limit_bytes=VMEM_LIMIT)


def _resident(shape, index_map):
    return pl.BlockSpec(shape, index_map, pipeline_mode=pl.Buffered(1))


def _inproj_kernel(x_ref, g_ref, w_ref, cos_ref, sin_ref, y_ref):
    x = x_ref[...]
    ms = jnp.mean(x * x, axis=-1, keepdims=True)
    h = (x * lax.rsqrt(ms + RMS_EPS)) * g_ref[...]
    y = jnp.dot(h.astype(BF16), w_ref[...], preferred_element_type=F32)
    cos = cos_ref[...]
    sin = sin_ref[...]
    lane = lax.broadcasted_iota(I32, (1, LANES), 1)
    first_half = (lane % HEAD_DIM) < (HEAD_DIM // 2)
    q_scale = HEAD_DIM ** -0.5
    for c0 in range(0, NW, LANES):
        yc = y[:, c0:c0 + LANES]
        if ROPE_LO <= c0 < ROPE_HI:
            partner = jnp.where(first_half, pltpu.roll(yc, LANES - HEAD_DIM // 2, 1),
                                pltpu.roll(yc, HEAD_DIM // 2, 1))
            r = yc * cos + partner * sin
            if c0 == C_KI:
                w_scaled = yc * jnp.where((lane >= C_WI - C_KI) & (lane < C_WI - C_KI + N_HEADS),
                                          N_HEADS ** -0.5, 1.0).astype(F32)
                r = jnp.where(lane < IDX_DIM, r, w_scaled)
            elif C_QA <= c0 < C_KA or C_QI <= c0 < C_KI:
                r = r * q_scale
            y_ref[:, c0:c0 + LANES] = r
        elif C_QB <= c0 < C_KB:
            y_ref[:, c0:c0 + LANES] = yc * q_scale
        else:
            y_ref[:, c0:c0 + LANES] = yc


def _in_projection(x_rows, g, w_perm, cos_tab, sin_tab, tab_index):
    rows = x_rows.shape[0]
    return pl.pallas_call(
        _inproj_kernel,
        grid=(rows // TR,),
        in_specs=[
            pl.BlockSpec((TR, D_MODEL), lambda i: (i, 0)),
            _resident((1, D_MODEL), lambda i: (0, 0)),
            _resident((D_MODEL, NW), lambda i: (0, 0)),
            pl.BlockSpec((TR, LANES), lambda i: (tab_index(i), 0)),
            pl.BlockSpec((TR, LANES), lambda i: (tab_index(i), 0)),
        ],
        out_specs=pl.BlockSpec((TR, NW), lambda i: (i, 0)),
        out_shape=jax.ShapeDtypeStruct((rows, NW), F32),
        compiler_params=_cparams(("arbitrary",)),
        name="in_projection",
    )(x_rows, g, w_perm, cos_tab, sin_tab)


def _sortable_key(score):
    bits = lax.bitcast_convert_type(score, I32)
    return bits ^ ((bits >> 31) & jnp.int32(0x7FFFFFFF))


def _topk_select(count_fn, shape, top_k, n_pos_bits):
    def bit_step(it, thr):
        cand = thr + (jnp.int32(1) << (31 - it))
        cnt = count_fn(lambda key, kpos: key >= cand)
        return jnp.where(cnt >= top_k, cand, thr)

    thr = lax.fori_loop(0, 32, bit_step, jnp.full(shape, INT_MIN, I32))
    c_ge = count_fn(lambda key, kpos: key >= thr)
    c_gt = count_fn(lambda key, kpos: key > thr)
    need = top_k - c_gt
    has_tie = (c_ge > top_k) & (thr > INT_MIN)

    def pos_step(it, x):
        cand = x + (jnp.int32(1) << (n_pos_bits - 1 - it))
        cnt = count_fn(lambda key, kpos: (key == thr) & (kpos < cand))
        return jnp.where(cnt < need, cand, x)

    def tie_search():
        return lax.fori_loop(0, n_pos_bits, pos_step, jnp.zeros(shape, I32))

    any_tie = jnp.max(has_tie.astype(I32)) > 0
    tie_x = lax.cond(any_tie, tie_search, lambda: jnp.zeros(shape, I32))
    big = jnp.int32(2 ** 30)
    tie_pos = jnp.where(thr == INT_MIN, -1, jnp.where(has_tie, tie_x, big))
    return thr, tie_pos


def _dsa_prompt_kernel(qam_ref, qit_ref, wit_ref, ka_ref, ki_ref, vt_ref, o_ref,
                       keys_sc, m_sc, l_sc, acc_sc, *, top_k, n_pos_bits):
    i = pl.program_id(1)
    nkb = i + 1
    row = lax.broadcasted_iota(I32, (TK, TQ), 0)
    col = lax.broadcasted_iota(I32, (TK, TQ), 1)
    wit = wit_ref[0, 0]

    def score_block(kb, carry):
        k_idx = ki_ref[0, pl.ds(pl.multiple_of(kb * TK, TK), TK), :]
        sc = jnp.zeros((TK, TQ), F32)
        for h in range(N_HEADS):
            r = jnp.dot(k_idx, qit_ref[0, 0, h * IDX_DIM:(h + 1) * IDX_DIM, :],
                        preferred_element_type=F32)
            sc = sc + wit[h:h + 1, :] * jnp.maximum(r, 0.0)
        keys_sc[kb] = _sortable_key(sc)
        return carry

    lax.fori_loop(0, nkb, score_block, 0)
    keys_sc[i] = jnp.where(row <= col, keys_sc[i], INT_MIN)

    def count_fn(pred):
        def body(kb, acc):
            m = pred(keys_sc[kb], kb * TK + row).astype(I32)
            for g in range(TK // SUBLANES):
                acc = acc + m[g * SUBLANES:(g + 1) * SUBLANES]
            return acc
        acc = lax.fori_loop(0, nkb, body, jnp.zeros((SUBLANES, TQ), I32))
        return jnp.sum(acc, axis=0, keepdims=True)

    thr, tie_pos = _topk_select(count_fn, (1, TQ), top_k, n_pos_bits)

    m_sc[...] = jnp.full(m_sc.shape, NEG_BIG, F32)
    l_sc[...] = jnp.zeros(l_sc.shape, F32)
    acc_sc[...] = jnp.zeros(acc_sc.shape, F32)

    def attend_block(kb, carry):
        key = keys_sc[kb]
        sel = (key > thr) | ((key == thr) & ((kb * TK + row) <= tie_pos))
        bias = jnp.where(sel, 0.0, NEG_BIG).astype(F32)
        start = pl.multiple_of(kb * TK, TK)
        for h in range(N_HEADS):
            pair = h // 2
            k_pair = ka_ref[0, pl.ds(start, TK), pair * LANES:(pair + 1) * LANES]
            s = jnp.dot(k_pair, qam_ref[0, 0, h], preferred_element_type=F32) + bias
            m_old = m_sc[h:h + 1, :]
            m_new = jnp.maximum(m_old, jnp.max(s, axis=0, keepdims=True))
            p = jnp.exp(s - m_new)
            alpha = jnp.exp(m_old - m_new)
            l_sc[h:h + 1, :] = alpha * l_sc[h:h + 1, :] + jnp.sum(p, axis=0, keepdims=True)
            pv = jnp.dot(vt_ref[0, kb, h * HEAD_DIM:(h + 1) * HEAD_DIM, :], p.astype(BF16),
                         preferred_element_type=F32)
            rows = slice(h * HEAD_DIM, (h + 1) * HEAD_DIM)
            acc_sc[rows, :] = alpha * acc_sc[rows, :] + pv
            m_sc[h:h + 1, :] = m_new
        return carry

    lax.fori_loop(0, nkb, attend_block, 0)
    for h in range(N_HEADS):
        rows = slice(h * HEAD_DIM, (h + 1) * HEAD_DIM)
        acc_sc[rows, :] = acc_sc[rows, :] / l_sc[h:h + 1, :]
    o_ref[0] = acc_sc[...].T


def _dsa_prompt(qam, qit, wit, ka, ki, vt, top_k):
    b, nqb = qam.shape[0], qam.shape[1]
    s_pad = ka.shape[1]
    nkb = s_pad // TK
    kern = functools.partial(_dsa_prompt_kernel, top_k=top_k, n_pos_bits=int(s_pad).bit_length())
    return pl.pallas_call(
        kern,
        grid=(b, nqb),
        in_specs=[
            pl.BlockSpec((1, 1, N_HEADS, LANES, TQ), lambda bb, i: (bb, i, 0, 0, 0)),
            pl.BlockSpec((1, 1, WIDTH, TQ), lambda bb, i: (bb, i, 0, 0)),
            pl.BlockSpec((1, 1, N_HEADS, TQ), lambda bb, i: (bb, i, 0, 0)),
            _resident((1, s_pad, WIDTH), lambda bb, i: (bb, 0, 0)),
            _resident((1, s_pad, IDX_DIM), lambda bb, i: (bb, 0, 0)),
            _resident((1, nkb, WIDTH, TK), lambda bb, i: (bb, 0, 0, 0)),
        ],
        out_specs=pl.BlockSpec((1, TQ, WIDTH), lambda bb, i: (bb, i, 0)),
        out_shape=jax.ShapeDtypeStruct((b, s_pad, WIDTH), F32),
        scratch_shapes=[
            pltpu.VMEM((nkb, TK, TQ), I32),
            pltpu.VMEM((N_HEADS, TQ), F32),
            pltpu.VMEM((N_HEADS, TQ), F32),
            pltpu.VMEM((WIDTH, TQ), F32),
        ],
        compiler_params=_cparams(("arbitrary", "arbitrary")),
        name="dsa_prompt",
    )(qam, qit, wit, ka, ki, vt)


def _neg_softplus(z):
    return -(jnp.maximum(z, 0.0) + jnp.log(1.0 + jnp.exp(-jnp.abs(z))))


def _sb_prompt_kernel(qbm_ref, kb_ref, vt_ref, o_ref, acc_sc):
    i = pl.program_id(1)
    row = lax.broadcasted_iota(I32, (TK, TQ), 0)
    col = lax.broadcasted_iota(I32, (TK, TQ), 1)
    key_off = RUN * (row % SUBLANES) + row // SUBLANES
    strict = key_off < col
    sub = lax.broadcasted_iota(I32, (SUBLANES, TQ), 0)

    def block(z, lk, carry):
        run_sums = []
        acc = jnp.zeros((SUBLANES, TQ), F32)
        for v in reversed(range(RUN)):
            acc = acc + lk[v * SUBLANES:(v + 1) * SUBLANES]
            run_sums.append(acc)
        run_sums = run_sums[::-1]
        tot = run_sums[0]
        later = jnp.zeros((SUBLANES, TQ), F32)
        for k in range(1, SUBLANES):
            shifted = pltpu.roll(tot, SUBLANES - k, 0)
            later = later + jnp.where(sub + k < SUBLANES, shifted, 0.0)
        offs = later + carry
        parts = [jnp.exp(z[v * SUBLANES:(v + 1) * SUBLANES] + run_sums[v] + offs) for v in range(RUN)]
        a = jnp.concatenate(parts, axis=0)
        return a, carry + jnp.sum(tot, axis=0, keepdims=True)

    for h in range(N_HEADS):
        pair = h // 2
        rows = slice(h * HEAD_DIM, (h + 1) * HEAD_DIM)
        q_m = qbm_ref[0, 0, h]

        start = pl.multiple_of(i * TK, TK)
        k_pair = kb_ref[0, pl.ds(start, TK), pair * LANES:(pair + 1) * LANES]
        z = jnp.dot(k_pair, q_m, preferred_element_type=F32)
        lk = jnp.where(strict, _neg_softplus(z), 0.0)
        a, carry = block(z, lk, jnp.zeros((1, TQ), F32))
        a = jnp.where(strict, a, 0.0)
        acc_sc[rows, :] = jnp.dot(vt_ref[0, i, rows, :], a.astype(BF16), preferred_element_type=F32)

        def earlier(j, carry):
            kb = i - 1 - j
            st = pl.multiple_of(kb * TK, TK)
            kp = kb_ref[0, pl.ds(st, TK), pair * LANES:(pair + 1) * LANES]
            zz = jnp.dot(kp, q_m, preferred_element_type=F32)
            aa, carry = block(zz, _neg_softplus(zz), carry)
            acc_sc[rows, :] += jnp.dot(vt_ref[0, kb, rows, :], aa.astype(BF16),
                                       preferred_element_type=F32)
            return carry

        lax.fori_loop(0, i, earlier, carry)
    o_ref[0] = acc_sc[...].T


def _sb_prompt(qbm, kb_perm, vt_perm):
    b, nqb = qbm.shape[0], qbm.shape[1]
    s_pad = kb_perm.shape[1]
    nkb = s_pad // TK
    return pl.pallas_call(
        _sb_prompt_kernel,
        grid=(b, nqb),
        in_specs=[
            pl.BlockSpec((1, 1, N_HEADS, LANES, TQ), lambda bb, i: (bb, i, 0, 0, 0)),
            _resident((1, s_pad, WIDTH), lambda bb, i: (bb, 0, 0)),
            _resident((1, nkb, WIDTH, TK), lambda bb, i: (bb, 0, 0, 0)),
        ],
        out_specs=pl.BlockSpec((1, TQ, WIDTH), lambda bb, i: (bb, i, 0)),
        out_shape=jax.ShapeDtypeStruct((b, s_pad, WIDTH), F32),
        scratch_shapes=[pltpu.VMEM((WIDTH, TQ), F32)],
        compiler_params=_cparams(("arbitrary", "arbitrary")),
        name="sb_prompt",
    )(qbm, kb_perm, vt_perm)


def _tile_rows(x, n):
    return jnp.concatenate([x] * n, axis=0)


def _idx_sample_kernel(pt_ref, qi_ref, wi_ref, kc_ref, kn_ref, bias_ref, keys_sc, *, n_pages, top_k,
                       n_new, n_pos_bits):
    p = pl.program_id(1)
    lane = lax.broadcasted_iota(I32, (n_new, LANES), 1)
    qrow = lax.broadcasted_iota(I32, (n_new, LANES), 0)

    def page_scores(k_page):
        r = lax.dot_general(qi_ref[0], k_page, (((1,), (1,)), ((), ())),
                            preferred_element_type=F32)
        rel = jnp.maximum(r, 0.0) * wi_ref[0]
        sc = jnp.zeros((n_new, LANES), F32)
        for h in range(N_HEADS):
            sc = sc + rel[h * n_new:(h + 1) * n_new]
        return _sortable_key(sc)

    @pl.when(p < n_pages)
    def _():
        keys_sc[p] = page_scores(kc_ref[0].astype(BF16))

    @pl.when(p == n_pages)
    def _():
        keys_sc[n_pages] = jnp.where(lane <= qrow, page_scores(kn_ref[0]), INT_MIN)

        def count_fn(pred):
            def body(pp, acc):
                return acc + pred(keys_sc[pp], pp * PAGE_SIZE + lane).astype(I32)
            acc = lax.fori_loop(0, n_pages + 1, body, jnp.zeros((n_new, LANES), I32))
            return jnp.sum(acc, axis=1, keepdims=True)

        thr, tie_pos = _topk_select(count_fn, (n_new, 1), top_k, n_pos_bits)

        def write(pp, carry):
            key = keys_sc[pp]
            sel = (key > thr) | ((key == thr) & ((pp * PAGE_SIZE + lane) <= tie_pos))
            bias_ref[0, pp] = jnp.where(sel, 0.0, NEG_BIG).astype(F32)
            return carry

        lax.fori_loop(0, n_pages + 1, write, 0)


def _att_sample_kernel(pt_ref, q_ref, bias_ref, kc_ref, vc_ref, kn_ref, vn_ref, o_ref,
                       m_sc, l_sc, acc_sc, *, n_pages, n_new):
    p = pl.program_id(1)

    @pl.when(p == 0)
    def _():
        m_sc[...] = jnp.full(m_sc.shape, NEG_BIG, F32)
        l_sc[...] = jnp.zeros(l_sc.shape, F32)
        acc_sc[...] = jnp.zeros(acc_sc.shape, F32)

    def process(k_page, v_page):
        s = lax.dot_general(q_ref[0], k_page, (((1,), (1,)), ((), ())),
                            preferred_element_type=F32)
        s = s + _tile_rows(bias_ref[0, 0], N_HEADS)
        m_old = m_sc[...]
        m_new = jnp.maximum(m_old, jnp.max(s, axis=1, keepdims=True))
        pr = jnp.exp(s - m_new)
        alpha = jnp.exp(m_old - m_new)
        l_sc[...] = alpha * l_sc[...] + jnp.sum(pr, axis=1, keepdims=True)
        acc_sc[...] = alpha * acc_sc[...] + jnp.dot(pr.astype(BF16), v_page, preferred_element_type=F32)
        m_sc[...] = m_new

    @pl.when(p < n_pages)
    def _():
        process(kc_ref[0].astype(BF16), vc_ref[0].astype(BF16))

    @pl.when(p == n_pages)
    def _():
        process(kn_ref[0], vn_ref[0])
        o = acc_sc[...] / l_sc[...]
        head_of_col = lax.broadcasted_iota(I32, (n_new, WIDTH), 1) // HEAD_DIM
        out = jnp.zeros((n_new, WIDTH), F32)
        for h in range(N_HEADS):
            out = jnp.where(head_of_col == h, o[h * n_new:(h + 1) * n_new], out)
        o_ref[0] = out


def _sb_sample_kernel(pt_ref, q_ref, kc_ref, vc_ref, kn_ref, vn_ref, o_ref, carry_sc, acc_sc,
                      *, n_pages, n_new):
    j = pl.program_id(1)
    rows = N_HEADS * n_new
    r_i = lax.broadcasted_iota(I32, (LANES, LANES), 0)
    c_i = lax.broadcasted_iota(I32, (LANES, LANES), 1)
    tri = jnp.where(r_i >= c_i, 1.0, 0.0).astype(BF16)

    def process(k_page, v_page, mask):
        z = lax.dot_general(q_ref[0], k_page, (((1,), (1,)), ((), ())),
                            preferred_element_type=F32)
        lk = _neg_softplus(z)
        if mask is not None:
            lk = jnp.where(mask, lk, 0.0)
        hi = lk.astype(BF16)
        lo = (lk - hi.astype(F32)).astype(BF16)
        suffix = (jnp.dot(hi, tri, preferred_element_type=F32)
                  + jnp.dot(lo, tri, preferred_element_type=F32))
        a = jnp.exp(z + suffix + carry_sc[...])
        if mask is not None:
            a = jnp.where(mask, a, 0.0)
        acc_sc[...] += jnp.dot(a.astype(BF16), v_page, preferred_element_type=F32)
        carry_sc[...] += jnp.sum(lk, axis=1, keepdims=True)

    @pl.when(j == 0)
    def _():
        carry_sc[...] = jnp.zeros(carry_sc.shape, F32)
        acc_sc[...] = jnp.zeros(acc_sc.shape, F32)
        lane = lax.broadcasted_iota(I32, (rows, LANES), 1)
        q_of_row = lax.broadcasted_iota(I32, (rows, LANES), 0) % n_new
        process(kn_ref[0], vn_ref[0], lane < q_of_row)

    @pl.when(j > 0)
    def _():
        process(kc_ref[0].astype(BF16), vc_ref[0].astype(BF16), None)

    @pl.when(j == n_pages)
    def _():
        o = acc_sc[...]
        head_of_col = lax.broadcasted_iota(I32, (n_new, WIDTH), 1) // HEAD_DIM
        out = jnp.zeros((n_new, WIDTH), F32)
        for h in range(N_HEADS):
            out = jnp.where(head_of_col == h, o[h * n_new:(h + 1) * n_new], out)
        o_ref[0] = out


def _sample_attention(page_table, qi_rows, wi_rows, qa_bd, qb_bd, new_pages, caches, top_k):
    db, n_pages = page_table.shape
    n_new = qi_rows.shape[1] // N_HEADS
    rows = N_HEADS * n_new
    pt_flat = page_table.reshape(-1)
    n_pos_bits = int(n_pages * PAGE_SIZE + PAGE_SIZE).bit_length()

    def fwd_page(b, p, pt):
        return (pt[b * n_pages + jnp.minimum(p, n_pages - 1)], 0, 0)

    def rev_page(b, j, pt):
        return (pt[b * n_pages + n_pages - jnp.maximum(j, 1)], 0, 0)

    per_b3 = lambda b, p, pt: (b, 0, 0)

    bias = pl.pallas_call(
        functools.partial(_idx_sample_kernel, n_pages=n_pages, top_k=top_k, n_new=n_new,
                          n_pos_bits=n_pos_bits),
        grid_spec=pltpu.PrefetchScalarGridSpec(
            num_scalar_prefetch=1,
            grid=(db, n_pages + 1),
            in_specs=[
                pl.BlockSpec((1, rows, IDX_DIM), per_b3),
                pl.BlockSpec((1, rows, LANES), per_b3),
                pl.BlockSpec((1, PAGE_SIZE, IDX_DIM), fwd_page),
                pl.BlockSpec((1, PAGE_SIZE, IDX_DIM), per_b3),
            ],
            out_specs=pl.BlockSpec((1, n_pages + 1, n_new, LANES), lambda b, p, pt: (b, 0, 0, 0)),
            scratch_shapes=[pltpu.VMEM((n_pages + 1, n_new, LANES), I32)],
        ),
        out_shape=jax.ShapeDtypeStruct((db, n_pages + 1, n_new, LANES), F32),
        compiler_params=_cparams(("arbitrary", "arbitrary")),
        name="idx_sample",
    )(pt_flat, qi_rows, wi_rows, caches["k_idx"], new_pages["k_idx"])

    oa = pl.pallas_call(
        functools.partial(_att_sample_kernel, n_pages=n_pages, n_new=n_new),
        grid_spec=pltpu.PrefetchScalarGridSpec(
            num_scalar_prefetch=1,
            grid=(db, n_pages + 1),
            in_specs=[
                pl.BlockSpec((1, rows, WIDTH), per_b3),
                pl.BlockSpec((1, 1, n_new, LANES), lambda b, p, pt: (b, p, 0, 0)),
                pl.BlockSpec((1, PAGE_SIZE, WIDTH), fwd_page),
                pl.BlockSpec((1, PAGE_SIZE, WIDTH), fwd_page),
                pl.BlockSpec((1, PAGE_SIZE, WIDTH), per_b3),
                pl.BlockSpec((1, PAGE_SIZE, WIDTH), per_b3),
            ],
            out_specs=pl.BlockSpec((1, n_new, WIDTH), per_b3),
            scratch_shapes=[pltpu.VMEM((rows, 1), F32), pltpu.VMEM((rows, 1), F32),
                            pltpu.VMEM((rows, WIDTH), F32)],
        ),
        out_shape=jax.ShapeDtypeStruct((db, n_new, WIDTH), F32),
        compiler_params=_cparams(("arbitrary", "arbitrary")),
        name="att_sample",
    )(pt_flat, qa_bd, bias, caches["k_a"], caches["v_a"], new_pages["k_a"], new_pages["v_a"])

    ob = pl.pallas_call(
        functools.partial(_sb_sample_kernel, n_pages=n_pages, n_new=n_new),
        grid_spec=pltpu.PrefetchScalarGridSpec(
            num_scalar_prefetch=1,
            grid=(db, n_pages + 1),
            in_specs=[
                pl.BlockSpec((1, rows, WIDTH), per_b3),
                pl.BlockSpec((1, PAGE_SIZE, WIDTH), rev_page),
                pl.BlockSpec((1, PAGE_SIZE, WIDTH), rev_page),
                pl.BlockSpec((1, PAGE_SIZE, WIDTH), per_b3),
                pl.BlockSpec((1, PAGE_SIZE, WIDTH), per_b3),
            ],
            out_specs=pl.BlockSpec((1, n_new, WIDTH), per_b3),
            scratch_shapes=[pltpu.VMEM((rows, 1), F32), pltpu.VMEM((rows, WIDTH), F32)],
        ),
        out_shape=jax.ShapeDtypeStruct((db, n_new, WIDTH), F32),
        compiler_params=_cparams(("arbitrary", "arbitrary")),
        name="sb_sample",
    )(pt_flat, qb_bd, caches["k_b"], caches["v_b"], new_pages["k_b"], new_pages["v_b"])
    return oa, ob


def _rms(x, g):
    ms = jnp.mean(x * x, axis=-1, keepdims=True)
    return (x * lax.rsqrt(ms + RMS_EPS)) * g


def _merge_kernel(x_ref, oa_ref, ob_ref, gate_ref, wa_ref, wb_ref, wo_ref, g_ref, x1_ref, h2_ref):
    pa = jnp.dot(oa_ref[...].astype(BF16), wa_ref[...], preferred_element_type=F32)
    pb = jnp.dot(ob_ref[...].astype(BF16), wb_ref[...], preferred_element_type=F32)
    ga = gate_ref[:, 0:D_MODEL]
    gb = gate_ref[:, D_MODEL:2 * D_MODEL]
    m = jax.nn.sigmoid(ga) * pa + jax.nn.sigmoid(gb) * pb
    x1 = x_ref[...] + jnp.dot(m.astype(BF16), wo_ref[...], preferred_element_type=F32)
    x1_ref[...] = x1
    h2_ref[...] = _rms(x1, g_ref[...]).astype(BF16)


def _merge(x_rows, oa, ob, y, wa, wb, wo, g_ffn):
    rows = x_rows.shape[0]
    row_blk = lambda n: pl.BlockSpec((TR, n), lambda i: (i, 0))
    return pl.pallas_call(
        _merge_kernel,
        grid=(rows // TR,),
        in_specs=[
            row_blk(D_MODEL), row_blk(WIDTH), row_blk(WIDTH),
            pl.BlockSpec((TR, 2 * D_MODEL), lambda i: (i, 0)),
            _resident((WIDTH, D_MODEL), lambda i: (0, 0)),
            _resident((WIDTH, D_MODEL), lambda i: (0, 0)),
            _resident((D_MODEL, D_MODEL), lambda i: (0, 0)),
            _resident((1, D_MODEL), lambda i: (0, 0)),
        ],
        out_specs=[row_blk(D_MODEL), row_blk(D_MODEL)],
        out_shape=[jax.ShapeDtypeStruct((rows, D_MODEL), F32),
                   jax.ShapeDtypeStruct((rows, D_MODEL), BF16)],
        compiler_params=_cparams(("arbitrary",)),
        name="merge",
    )(x_rows, oa, ob, y, wa, wb, wo, g_ffn)


def _ffn_kernel(*refs, seq_len, slab, has_state):
    if has_state:
        (x1_ref, h2_ref, wup_ref, cw_ref, cb_ref, wdn_ref, gf_ref, s1_ref, s2_ref,
         y_ref, a_ref, buf) = refs
    else:
        x1_ref, h2_ref, wup_ref, cw_ref, cb_ref, wdn_ref, gf_ref, y_ref, a_ref, buf = refs
    t = pl.program_id(1)
    u = jnp.dot(h2_ref[0], wup_ref[...], preferred_element_type=F32)
    a = u[:, :D_FF]
    b = u[:, D_FF:]

    @pl.when(t == 0)
    def _():
        buf[0:SUBLANES, :] = jnp.zeros((SUBLANES, D_FF), F32)

    buf[SUBLANES:SUBLANES + TR, :] = a
    prev1 = buf[SUBLANES - 1:SUBLANES - 1 + TR, :]
    prev2 = buf[SUBLANES - 2:SUBLANES - 2 + TR, :]
    if has_state:
        tok = lax.broadcasted_iota(I32, (TR, 1), 0) % seq_len
        prev1 = jnp.where(tok == 0, s1_ref[...], prev1)
        prev2 = jnp.where(tok <= 1, s2_ref[...], prev2)
    cw = cw_ref[...]
    a_c = cb_ref[...] + cw[0:1, :] * prev2 + cw[1:2, :] * prev1 + cw[2:3, :] * a
    gelu = 0.5 * a_c * (1.0 + lax.erf(a_c * (2.0 ** -0.5)))
    f = jnp.dot((gelu * b).astype(BF16), wdn_ref[...], preferred_element_type=F32)
    y_ref[0] = _rms(x1_ref[0] + f, gf_ref[...])
    a_ref[0] = a[slab[0]:slab[0] + slab[1], :]
    buf[0:SUBLANES, :] = a[TR - SUBLANES:, :]


def _ffn(x1, h2, w_up, conv_w, conv_b, w_down, g_final, slab, state=None, seq_len=0):
    nseq, rows = x1.shape[0], x1.shape[1]
    nt = rows // TR
    blk = lambda n: pl.BlockSpec((1, TR, n), lambda s, t: (s, t, 0))
    const = lambda shape: _resident(shape, lambda s, t: (0,) * len(shape))
    in_specs = [blk(D_MODEL), blk(D_MODEL), const((D_MODEL, 2 * D_FF)), const((3, D_FF)),
                const((1, D_FF)), const((D_FF, D_MODEL)), const((1, D_MODEL))]
    args = [x1, h2, w_up, conv_w, conv_b, w_down, g_final]
    if state is not None:
        in_specs += [const((TR, D_FF)), const((TR, D_FF))]
        args += list(state)
    return pl.pallas_call(
        functools.partial(_ffn_kernel, seq_len=seq_len, slab=slab, has_state=state is not None),
        grid=(nseq, nt),
        in_specs=in_specs,
        out_specs=[blk(D_MODEL), pl.BlockSpec((1, slab[1], D_FF), lambda s, t: (s, t, 0))],
        out_shape=[jax.ShapeDtypeStruct((nseq, rows, D_MODEL), F32),
                   jax.ShapeDtypeStruct((nseq, nt * slab[1], D_FF), F32)],
        scratch_shapes=[pltpu.VMEM((SUBLANES + TR, D_FF), F32)],
        compiler_params=_cparams(("arbitrary", "arbitrary")),
        name="conv_ffn",
    )(*args)


def _rope_tables(positions):
    half = HEAD_DIM // 2
    inv = ROPE_THETA ** (-np.arange(half, dtype=np.float64) * 2.0 / HEAD_DIM)
    ang = np.asarray(positions, np.float64)[:, None] * inv[None, :]
    cos = np.tile(np.cos(ang), (1, LANES // half))
    sin = np.sin(ang)
    sin = np.tile(np.concatenate([-sin, sin], axis=1), (1, LANES // HEAD_DIM))
    return cos.astype(np.float32), sin.astype(np.float32)


def _permute_w_in(w):
    off = np.cumsum([0, WIDTH, WIDTH, WIDTH, WIDTH, IDX_DIM, N_HEADS, WIDTH, WIDTH, WIDTH,
                     D_MODEL, D_MODEL])
    qa, ka, va, qi, ki, wi, qb, kb, vb, ga, gb = [w[:, off[j]:off[j + 1]] for j in range(11)]
    pad = jnp.zeros((w.shape[0], C_VA - C_WI - N_HEADS), w.dtype)
    out = jnp.concatenate([ga, gb, qa, ka, qi, ki, wi, pad, va, qb, kb, vb], axis=1)
    assert out.shape[1] == NW
    return out.astype(BF16)


def _blocked_t(x, blk):
    b, s, c = x.shape
    return x.reshape(b, s // blk, blk, c).transpose(0, 1, 3, 2)


def _masked_pairs_t(q):
    b, s, _ = q.shape
    qh = q.reshape(b, s, N_HEADS // 2, 2, HEAD_DIM)
    z = jnp.zeros_like(qh[:, :, :, 0])
    even = jnp.concatenate([qh[:, :, :, 0], z], axis=-1)
    odd = jnp.concatenate([z, qh[:, :, :, 1]], axis=-1)
    qm = jnp.stack([even, odd], axis=3).reshape(b, s, N_HEADS, LANES)
    return qm.reshape(b, s // TQ, TQ, N_HEADS, LANES).transpose(0, 1, 3, 4, 2)


def _permute_runs(x):
    b, s, c = x.shape
    return x.reshape(b, s // TK, SUBLANES, RUN, c).transpose(0, 1, 3, 2, 4).reshape(b, s, c)


def _block_diag_rows(q):
    db, n_new, _ = q.shape
    head_of_col = jnp.arange(WIDTH) // HEAD_DIM
    keep = head_of_col[None, :] == jnp.arange(N_HEADS)[:, None]
    out = jnp.where(keep[None, :, None, :], q[:, None, :, :], 0)
    return out.reshape(db, N_HEADS * n_new, WIDTH)


def _pad_page(x):
    return jnp.pad(x, ((0, 0), (0, PAGE_SIZE - x.shape[1]), (0, 0)))


def kernel(x_prompt, x_sample, cache_k_a, cache_v_a, cache_k_idx, cache_k_b, cache_v_b, state_conv,
           page_table, meta_tokens, g_attn, w_in, w_branch_a, w_branch_b, w_o, g_ffn, w_up, conv_w,
           conv_b, w_down, g_final):
    batch, seq = x_prompt.shape[0], x_prompt.shape[1]
    db, n_new = x_sample.shape[0], x_sample.shape[1]
    n_pages = page_table.shape[1]
    past_len = n_pages * PAGE_SIZE
    s_len = N_META + seq
    s_pad = -(-s_len // TQ) * TQ
    nqb = s_pad // TQ
    top_k_p = min(TOP_K_MAX, seq // 4)
    top_k_s = min(TOP_K_MAX, (past_len + n_new) // 4)
    n_samp_rows = db * n_new
    assert n_samp_rows == TR and cache_k_a.shape[0] == 1 and TR % n_new == 0
    assert (s_len - 2) // SUBLANES == (s_len - 1) // SUBLANES

    meta = jnp.broadcast_to(meta_tokens[None], (batch, N_META, D_MODEL))
    xp = jnp.concatenate([meta, x_prompt, jnp.zeros((batch, s_pad - s_len, D_MODEL), F32)], axis=1)
    x_rows = jnp.concatenate([xp.reshape(batch * s_pad, D_MODEL),
                              x_sample.reshape(n_samp_rows, D_MODEL)], axis=0)
    pos = np.concatenate([np.arange(s_pad), past_len + np.arange(TR) % n_new])
    cos_tab, sin_tab = _rope_tables(pos)
    n_prompt_tiles = batch * nqb
    tab_index = lambda i: jnp.where(i < n_prompt_tiles, i % nqb, nqb)

    y = _in_projection(x_rows, g_attn, _permute_w_in(w_in[0]), jnp.asarray(cos_tab),
                       jnp.asarray(sin_tab), tab_index)
    yp = y[:batch * s_pad].reshape(batch, s_pad, NW)
    ys = y[batch * s_pad:].reshape(db, n_new, NW)
    col = lambda a, c0, n: a[..., c0:c0 + n]

    ka_p, va_p, ki_p = col(yp, C_KA, WIDTH), col(yp, C_VA, WIDTH), col(yp, C_KI, IDX_DIM)
    kb_p, vb_p = col(yp, C_KB, WIDTH), col(yp, C_VB, WIDTH)
    oa_p = _dsa_prompt(
        _masked_pairs_t(col(yp, C_QA, WIDTH).astype(BF16)),
        _blocked_t(col(yp, C_QI, WIDTH).astype(BF16), TQ),
        _blocked_t(col(yp, C_WI, N_HEADS), TQ),
        ka_p.astype(BF16), ki_p.astype(BF16), _blocked_t(va_p.astype(BF16), TK), top_k_p)
    ob_p = _sb_prompt(
        _masked_pairs_t(col(yp, C_QB, WIDTH).astype(BF16)),
        _permute_runs(kb_p.astype(BF16)),
        _blocked_t(_permute_runs(vb_p.astype(BF16)), TK))

    ka_s, va_s, ki_s = col(ys, C_KA, WIDTH), col(ys, C_VA, WIDTH), col(ys, C_KI, IDX_DIM)
    kb_s, vb_s = col(ys, C_KB, WIDTH), col(ys, C_VB, WIDTH)
    qi_s = col(ys, C_QI, WIDTH).astype(BF16).reshape(db, n_new, N_HEADS, IDX_DIM)
    qi_rows = qi_s.transpose(0, 2, 1, 3).reshape(db, N_HEADS * n_new, IDX_DIM)
    wi_rows = jnp.broadcast_to(
        col(ys, C_WI, N_HEADS).transpose(0, 2, 1).reshape(db, N_HEADS * n_new, 1),
        (db, N_HEADS * n_new, LANES))
    n_phys = cache_k_a.shape[1]
    caches = {
        "k_a": cache_k_a.reshape(n_phys, PAGE_SIZE, WIDTH),
        "v_a": cache_v_a.reshape(n_phys, PAGE_SIZE, WIDTH),
        "k_idx": cache_k_idx.reshape(n_phys, PAGE_SIZE, IDX_DIM),
        "k_b": cache_k_b.reshape(n_phys, PAGE_SIZE, WIDTH),
        "v_b": cache_v_b.reshape(n_phys, PAGE_SIZE, WIDTH),
    }
    new_pages = {name: _pad_page(arr.astype(BF16)) for name, arr in
                 (("k_a", ka_s), ("v_a", va_s), ("k_idx", ki_s), ("k_b", kb_s), ("v_b", vb_s))}
    oa_s, ob_s = _sample_attention(
        page_table, qi_rows, wi_rows,
        _block_diag_rows(col(ys, C_QA, WIDTH).astype(BF16)),
        _block_diag_rows(col(ys, C_QB, WIDTH).astype(BF16)),
        new_pages, caches, top_k_s)

    oa_rows = jnp.concatenate([oa_p.reshape(batch * s_pad, WIDTH), oa_s.reshape(n_samp_rows, WIDTH)], 0)
    ob_rows = jnp.concatenate([ob_p.reshape(batch * s_pad, WIDTH), ob_s.reshape(n_samp_rows, WIDTH)], 0)
    x1, h2 = _merge(x_rows, oa_rows, ob_rows, y, w_branch_a[0].astype(BF16),
                    w_branch_b[0].astype(BF16), w_o[0].astype(BF16), g_ffn)
    ffn_w = (w_up[0].astype(BF16), conv_w[0], conv_b, w_down[0].astype(BF16), g_final[None])
    tail_tile, tail_off = (s_len - 2) // TR, (s_len - 2) % TR
    slab_off = tail_off // SUBLANES * SUBLANES
    y_p, a_p = _ffn(x1[:batch * s_pad].reshape(batch, s_pad, D_MODEL),
                    h2[:batch * s_pad].reshape(batch, s_pad, D_MODEL), *ffn_w,
                    slab=(slab_off, SUBLANES))
    st = state_conv[0]
    zeros = jnp.zeros((db, n_new - 1, D_FF), F32)
    s1 = jnp.concatenate([st[:, 1:2], zeros], axis=1).reshape(TR, D_FF)
    s2 = jnp.concatenate([st[:, 0:1], st[:, 1:2], zeros[:, 1:]], axis=1).reshape(TR, D_FF)
    y_s, a_s = _ffn(x1[batch * s_pad:][None], h2[batch * s_pad:][None], *ffn_w,
                    slab=(0, TR), state=(s1, s2), seq_len=n_new)

    y_prompt = y_p[:, N_META:s_len]
    y_sample = y_s.reshape(db, n_new, D_MODEL)
    a_tail = tail_tile * SUBLANES + tail_off - slab_off
    conv_p = a_p[:, a_tail:a_tail + 2][None]
    conv_s = a_s.reshape(db, n_new, D_FF)[:, n_new - 2:][None]
    heads = lambda a, n: a[:, :n].reshape(1, a.shape[0], n, N_HEADS, HEAD_DIM)
    return (y_prompt, y_sample,
            heads(ka_p, s_len), heads(va_p, s_len), ki_p[:, :s_len][None],
            heads(kb_p, s_len), heads(vb_p, s_len), conv_p,
            heads(ka_s, n_new), heads(va_s, n_new), ki_s[None],
            heads(kb_s, n_new), heads(vb_s, n_new), conv_s)
```

```python
import functools

import numpy as np
import jax
import jax.numpy as jnp
from jax import lax
from jax.experimental import pallas as pl
from jax.experimental.pallas import tpu as pltpu

F32 = jnp.float32
BF16 = jnp.bfloat16
I32 = jnp.int32

D_MODEL = 1024
N_HEADS = 8
HEAD_DIM = 64
WIDTH = N_HEADS * HEAD_DIM
IDX_DIM = 64
N_META = 16
PAGE_SIZE = 128
TOP_K_MAX = 256
ROPE_THETA = 10000.0
D_FF = 2816
RMS_EPS = 1e-6

LANES = 128
SUBLANES = 8
TQ = 256
TK = 256
TR = 256
RUN = TK // SUBLANES
PAGES_PER_STEP = 16
VMEM_LIMIT = 56 * 1024 * 1024

INT_MIN = -2 ** 31
NEG_BIG = -1e30
EXP_ZERO = -104.0

C_GA, C_GB = 0, 1024
C_QA, C_KA, C_QI = 2048, 2560, 3072
C_KI, C_WI = 3584, 3648
C_VA, C_QB, C_KB, C_VB = 3712, 4224, 4736, 5248
NW = 5760
ROPE_LO, ROPE_HI = C_QA, C_VA


def _cparams(sem):
    return pltpu.CompilerParams(dimension_semantics=sem, vmem_limit_bytes=VMEM_LIMIT)


def _resident(shape, index_map):
    return pl.BlockSpec(shape, index_map, pipeline_mode=pl.Buffered(1))


def _inproj_kernel(x_ref, g_ref, w_ref, cos_ref, sin_ref,
                   ka_f, va_f, ki_f, kb_f, vb_f,
                   qa_b, qi_b, qb_b, ka_b, va_b, kb_b, vb_b, ki_b, kiwi_f, gate_f):
    x = x_ref[0]
    ms = jnp.mean(x * x, axis=-1, keepdims=True)
    h = (x * lax.rsqrt(ms + RMS_EPS)) * g_ref[...]
    y = jnp.dot(h.astype(BF16), w_ref[...], preferred_element_type=F32)
    cos = cos_ref[...]
    sin = sin_ref[...]
    lane = lax.broadcasted_iota(I32, (1, LANES), 1)
    first_half = (lane % HEAD_DIM) < (HEAD_DIM // 2)
    q_scale = HEAD_DIM ** -0.5
    w_lanes = (lane >= C_WI - C_KI) & (lane < C_WI - C_KI + N_HEADS)

    def rope(yc):
        partner = jnp.where(first_half, pltpu.roll(yc, LANES - HEAD_DIM // 2, 1),
                            pltpu.roll(yc, HEAD_DIM // 2, 1))
        return yc * cos + partner * sin

    plan = [(C_GA, 2 * D_MODEL, False, 1.0, gate_f, None),
            (C_QA, WIDTH, True, q_scale, None, qa_b), (C_KA, WIDTH, True, 1.0, ka_f, ka_b),
            (C_QI, WIDTH, True, q_scale, None, qi_b),
            (C_VA, WIDTH, False, 1.0, va_f, va_b), (C_QB, WIDTH, False, q_scale, None, qb_b),
            (C_KB, WIDTH, False, 1.0, kb_f, kb_b), (C_VB, WIDTH, False, 1.0, vb_f, vb_b)]
    for c_lo, width, rotary, scale, dst_f, dst_b in plan:
        for off in range(0, width, LANES):
            r = y[:, c_lo + off:c_lo + off + LANES]
            if rotary:
                r = rope(r)
            if scale != 1.0:
                r = r * scale
            if dst_f is not None:
                dst_f[0, :, off:off + LANES] = r
            if dst_b is not None:
                dst_b[0, :, off:off + LANES] = r.astype(BF16)
    yc = y[:, C_KI:C_KI + LANES]
    r = jnp.where(lane < IDX_DIM, rope(yc), yc * jnp.where(w_lanes, N_HEADS ** -0.5, 1.0).astype(F32))
    kiwi_f[0] = r
    ki_f[0] = r[:, :IDX_DIM]
    ki_b[0] = r[:, :IDX_DIM].astype(BF16)


def _in_projection(x3, g, w_perm, cos_tab, sin_tab, tab_index, leaf_len):
    groups, rows, _ = x3.shape
    blk = lambda n: pl.BlockSpec((1, TR, n), lambda b, t: (b, t, 0))
    leaf = lambda n: jax.ShapeDtypeStruct((groups, leaf_len, n), F32)
    full = lambda n, dt: jax.ShapeDtypeStruct((groups, rows, n), dt)
    tab_spec = pl.BlockSpec((TR, LANES), lambda b, t: (tab_index(b, t), 0))
    return pl.pallas_call(
        _inproj_kernel,
        grid=(groups, rows // TR),
        in_specs=[blk(D_MODEL), _resident((1, D_MODEL), lambda b, t: (0, 0)),
                  _resident((D_MODEL, NW), lambda b, t: (0, 0)), tab_spec, tab_spec],
        out_specs=[blk(WIDTH), blk(WIDTH), blk(IDX_DIM), blk(WIDTH), blk(WIDTH)]
                  + [blk(WIDTH)] * 7 + [blk(IDX_DIM), blk(LANES), blk(2 * D_MODEL)],
        out_shape=[leaf(WIDTH), leaf(WIDTH), leaf(IDX_DIM), leaf(WIDTH), leaf(WIDTH)]
                  + [full(WIDTH, BF16)] * 7 + [full(IDX_DIM, BF16), full(LANES, F32),
                                               full(2 * D_MODEL, F32)],
        compiler_params=_cparams(("arbitrary", "arbitrary")),
        name="in_projection",
    )(x3, g, w_perm, cos_tab, sin_tab)


def _sortable_key(score):
    bits = lax.bitcast_convert_type(score, I32)
    return bits ^ ((bits >> 31) & jnp.int32(0x7FFFFFFF))


def _topk_select(count_fn, shape, top_k, n_pos_bits):
    def bit_step(it, thr):
        cand = thr + (jnp.int32(1) << (31 - it))
        cnt = count_fn(lambda key, kpos: key >= cand)
        return jnp.where(cnt >= top_k, cand, thr)

    thr = lax.fori_loop(0, 32, bit_step, jnp.full(shape, INT_MIN, I32))
    c_ge = count_fn(lambda key, kpos: key >= thr)
    c_gt = count_fn(lambda key, kpos: key > thr)
    need = top_k - c_gt
    has_tie = (c_ge > top_k) & (thr > INT_MIN)

    def pos_step(it, x):
        cand = x + (jnp.int32(1) << (n_pos_bits - 1 - it))
        cnt = count_fn(lambda key, kpos: (key == thr) & (kpos < cand))
        return jnp.where(cnt < need, cand, x)

    def tie_search():
        return lax.fori_loop(0, n_pos_bits, pos_step, jnp.zeros(shape, I32))

    any_tie = jnp.max(has_tie.astype(I32)) > 0
    tie_x = lax.cond(any_tie, tie_search, lambda: jnp.zeros(shape, I32))
    big = jnp.int32(2 ** 30)
    tie_pos = jnp.where(thr == INT_MIN, -1, jnp.where(has_tie, tie_x, big))
    return thr, tie_pos


def _masked_pairs(q_t, qm_sc):
    zeros = jnp.zeros((HEAD_DIM, q_t.shape[1]), F32)
    for h in range(N_HEADS):
        part = q_t[h * HEAD_DIM:(h + 1) * HEAD_DIM]
        pair = [part, zeros] if h % 2 == 0 else [zeros, part]
        qm_sc[h] = jnp.concatenate(pair, axis=0).astype(BF16)


def _dsa_prompt_kernel(qa_ref, qi_ref, kiwi_ref, ka_ref, ki_ref, va_ref, o_ref,
                       keys_sc, vt_sc, qm_sc, qit_sc, m_sc, l_sc, acc_sc,
                       *, top_k, n_pos_bits, n_key_blocks):
    i = pl.program_id(1)
    nkb = i + 1
    row = lax.broadcasted_iota(I32, (TK, TQ), 0)
    col = lax.broadcasted_iota(I32, (TK, TQ), 1)

    @pl.when(i == 0)
    def _():
        def transpose_block(kb, carry):
            blk = va_ref[0, pl.ds(pl.multiple_of(kb * TK, TK), TK), :]
            vt_sc[kb] = blk.astype(F32).T.astype(BF16)
            return carry
        lax.fori_loop(0, n_key_blocks, transpose_block, 0)

    _masked_pairs(qa_ref[0].astype(F32).T, qm_sc)
    qit_sc[...] = qi_ref[0].astype(F32).T.astype(BF16)
    wit = kiwi_ref[0].T[C_WI - C_KI:C_WI - C_KI + N_HEADS, :]

    def score_block(kb, carry):
        k_idx = ki_ref[0, pl.ds(pl.multiple_of(kb * TK, TK), TK), :]
        sc = jnp.zeros((TK, TQ), F32)
        for h in range(N_HEADS):
            r = jnp.dot(k_idx, qit_sc[h * IDX_DIM:(h + 1) * IDX_DIM, :], preferred_element_type=F32)
            sc = sc + wit[h:h + 1, :] * jnp.maximum(r, 0.0)
        keys_sc[kb] = _sortable_key(sc)
        return carry

    lax.fori_loop(0, nkb, score_block, 0)
    keys_sc[i] = jnp.where(row <= col, keys_sc[i], INT_MIN)

    def count_fn(pred):
        def body(kb, acc):
            m = pred(keys_sc[kb], kb * TK + row).astype(I32)
            for g in range(TK // SUBLANES):
                acc = acc + m[g * SUBLANES:(g + 1) * SUBLANES]
            return acc
        acc = lax.fori_loop(0, nkb, body, jnp.zeros((SUBLANES, TQ), I32))
        return jnp.sum(acc, axis=0, keepdims=True)

    thr, tie_pos = _topk_select(count_fn, (1, TQ), top_k, n_pos_bits)

    m_sc[...] = jnp.full(m_sc.shape, NEG_BIG, F32)
    l_sc[...] = jnp.zeros(l_sc.shape, F32)
    acc_sc[...] = jnp.zeros(acc_sc.shape, F32)

    def attend_block(kb, carry):
        key = keys_sc[kb]
        sel = (key > thr) | ((key == thr) & ((kb * TK + row) <= tie_pos))
        bias = jnp.where(sel, 0.0, NEG_BIG).astype(F32)
        start = pl.multiple_of(kb * TK, TK)
        for h in range(N_HEADS):
            pair = h // 2
            k_pair = ka_ref[0, pl.ds(start, TK), pair * LANES:(pair + 1) * LANES]
            s = jnp.dot(k_pair, qm_sc[h], preferred_element_type=F32) + bias
            m_old = m_sc[h:h + 1, :]
            m_new = jnp.maximum(m_old, jnp.max(s, axis=0, keepdims=True))
            p = jnp.exp(s - m_new)
            alpha = jnp.exp(m_old - m_new)
            l_sc[h:h + 1, :] = alpha * l_sc[h:h + 1, :] + jnp.sum(p, axis=0, keepdims=True)
            pv = jnp.dot(vt_sc[kb, h * HEAD_DIM:(h + 1) * HEAD_DIM, :], p.astype(BF16),
                         preferred_element_type=F32)
            rows = slice(h * HEAD_DIM, (h + 1) * HEAD_DIM)
            acc_sc[rows, :] = alpha * acc_sc[rows, :] + pv
            m_sc[h:h + 1, :] = m_new
        return carry

    lax.fori_loop(0, nkb, attend_block, 0)
    for h in range(N_HEADS):
        rows = slice(h * HEAD_DIM, (h + 1) * HEAD_DIM)
        acc_sc[rows, :] = acc_sc[rows, :] / l_sc[h:h + 1, :]
    o_ref[0] = acc_sc[...].T


def _dsa_prompt(qa, qi, kiwi, ka, ki, va, top_k):
    b, s_pad, _ = ka.shape
    nkb = s_pad // TK
    kern = functools.partial(_dsa_prompt_kernel, top_k=top_k, n_pos_bits=int(s_pad).bit_length(),
                             n_key_blocks=nkb)
    q_blk = lambda n: pl.BlockSpec((1, TQ, n), lambda bb, i: (bb, i, 0))
    return pl.pallas_call(
        kern,
        grid=(b, s_pad // TQ),
        in_specs=[q_blk(WIDTH), q_blk(WIDTH), q_blk(LANES),
                  _resident((1, s_pad, WIDTH), lambda bb, i: (bb, 0, 0)),
                  _resident((1, s_pad, IDX_DIM), lambda bb, i: (bb, 0, 0)),
                  _resident((1, s_pad, WIDTH), lambda bb, i: (bb, 0, 0))],
        out_specs=q_blk(WIDTH),
        out_shape=jax.ShapeDtypeStruct((b, s_pad, WIDTH), F32),
        scratch_shapes=[
            pltpu.VMEM((nkb, TK, TQ), I32),
            pltpu.VMEM((nkb, WIDTH, TK), BF16),
            pltpu.VMEM((N_HEADS, LANES, TQ), BF16),
            pltpu.VMEM((WIDTH, TQ), BF16),
            pltpu.VMEM((N_HEADS, TQ), F32),
            pltpu.VMEM((N_HEADS, TQ), F32),
            pltpu.VMEM((WIDTH, TQ), F32),
        ],
        compiler_params=_cparams(("arbitrary", "arbitrary")),
        name="dsa_prompt",
    )(qa, qi, kiwi, ka, ki, va)


def _neg_softplus(z):
    return -(jnp.maximum(z, 0.0) + jnp.log(1.0 + jnp.exp(-jnp.abs(z))))


def _sb_prompt_kernel(qb_ref, kb_ref, vb_ref, o_ref,
                      kperm_sc, vt_sc, stage_sc, qm_sc, kinf_sm, acc_sc, *, n_key_blocks):
    i = pl.program_id(1)
    row = lax.broadcasted_iota(I32, (TK, TQ), 0)
    col = lax.broadcasted_iota(I32, (TK, TQ), 1)
    key_off = RUN * (row % SUBLANES) + row // SUBLANES
    strict = key_off < col
    sub = lax.broadcasted_iota(I32, (SUBLANES, TQ), 0)

    def permuted(block_f32):
        cols = []
        for c in range(WIDTH // LANES):
            stage_sc[c] = block_f32[:, c * LANES:(c + 1) * LANES]
            cols.append(jnp.concatenate(
                [stage_sc[c, pl.ds(v, SUBLANES, stride=RUN), :] for v in range(RUN)], axis=0))
        return jnp.concatenate(cols, axis=1)

    @pl.when(i == 0)
    def _():
        def prep(kb, kmax):
            start = pl.multiple_of(kb * TK, TK)
            k_blk = kb_ref[0, pl.ds(start, TK), :].astype(F32)
            k_abs = jnp.abs(k_blk)
            for g in range(TK // SUBLANES):
                kmax = jnp.maximum(kmax, k_abs[g * SUBLANES:(g + 1) * SUBLANES])
            kperm_sc[pl.ds(start, TK), :] = permuted(k_blk).astype(BF16)
            vt_sc[kb] = permuted(vb_ref[0, pl.ds(start, TK), :].astype(F32)).T.astype(BF16)
            return kmax
        kmax = lax.fori_loop(0, n_key_blocks, prep, jnp.zeros((SUBLANES, WIDTH), F32))
        col_max = jnp.max(kmax, axis=0, keepdims=True)
        for h in range(N_HEADS):
            kinf_sm[h] = jnp.max(col_max[:, h * HEAD_DIM:(h + 1) * HEAD_DIM])

    q_t = qb_ref[0].astype(F32).T
    _masked_pairs(q_t, qm_sc)

    def block(z, lk, carry):
        run_sums = []
        acc = jnp.zeros((SUBLANES, TQ), F32)
        for v in reversed(range(RUN)):
            acc = acc + lk[v * SUBLANES:(v + 1) * SUBLANES]
            run_sums.append(acc)
        run_sums = run_sums[::-1]
        tot = run_sums[0]
        later = jnp.zeros((SUBLANES, TQ), F32)
        for k in range(1, SUBLANES):
            shifted = pltpu.roll(tot, SUBLANES - k, 0)
            later = later + jnp.where(sub + k < SUBLANES, shifted, 0.0)
        offs = later + carry
        parts = [jnp.exp(z[v * SUBLANES:(v + 1) * SUBLANES] + run_sums[v] + offs) for v in range(RUN)]
        return jnp.concatenate(parts, axis=0), carry + jnp.sum(tot, axis=0, keepdims=True)

    for pair in range(N_HEADS // 2):
        heads = (2 * pair, 2 * pair + 1)
        lanes = slice(pair * LANES, (pair + 1) * LANES)
        bounds = [jnp.sum(jnp.abs(q_t[h * HEAD_DIM:(h + 1) * HEAD_DIM]), axis=0, keepdims=True)
                  * (kinf_sm[h] * (1.0 + 2.0 ** -8)) for h in heads]

        def live(carries):
            worst = jnp.maximum(carries[0] + bounds[0], carries[1] + bounds[1])
            return jnp.max(worst) > EXP_ZERO

        start = pl.multiple_of(i * TK, TK)
        k_pair = kperm_sc[pl.ds(start, TK), lanes]
        carries = []
        for h in heads:
            rows = slice(h * HEAD_DIM, (h + 1) * HEAD_DIM)
            z = jnp.dot(k_pair, qm_sc[h], preferred_element_type=F32)
            lk = jnp.where(strict, _neg_softplus(z), 0.0)
            a, carry = block(z, lk, jnp.zeros((1, TQ), F32))
            a = jnp.where(strict, a, 0.0)
            acc_sc[rows, :] = jnp.dot(vt_sc[i, rows, :], a.astype(BF16), preferred_element_type=F32)
            carries.append(carry)

        def cond(state):
            kb, _, _, go = state
            return (kb >= 0) & go

        def earlier(state):
            kb, c0, c1, _ = state
            st = pl.multiple_of(kb * TK, TK)
            kp = kperm_sc[pl.ds(st, TK), lanes]
            new = []
            for h, carry in zip(heads, (c0, c1)):
                rows = slice(h * HEAD_DIM, (h + 1) * HEAD_DIM)
                zz = jnp.dot(kp, qm_sc[h], preferred_element_type=F32)
                aa, carry = block(zz, _neg_softplus(zz), carry)
                acc_sc[rows, :] += jnp.dot(vt_sc[kb, rows, :], aa.astype(BF16),
                                           preferred_element_type=F32)
                new.append(carry)
            return kb - 1, new[0], new[1], live(new)

        lax.while_loop(cond, earlier, (i - 1, carries[0], carries[1], live(carries)))
    o_ref[0] = acc_sc[...].T


def _sb_prompt(qb, kb, vb):
    b, s_pad, _ = kb.shape
    nkb = s_pad // TK
    return pl.pallas_call(
        functools.partial(_sb_prompt_kernel, n_key_blocks=nkb),
        grid=(b, s_pad // TQ),
        in_specs=[pl.BlockSpec((1, TQ, WIDTH), lambda bb, i: (bb, i, 0)),
                  _resident((1, s_pad, WIDTH), lambda bb, i: (bb, 0, 0)),
                  _resident((1, s_pad, WIDTH), lambda bb, i: (bb, 0, 0))],
        out_specs=pl.BlockSpec((1, TQ, WIDTH), lambda bb, i: (bb, i, 0)),
        out_shape=jax.ShapeDtypeStruct((b, s_pad, WIDTH), F32),
        scratch_shapes=[
            pltpu.VMEM((s_pad, WIDTH), BF16),
            pltpu.VMEM((nkb, WIDTH, TK), BF16),
            pltpu.VMEM((WIDTH // LANES, TK, LANES), F32),
            pltpu.VMEM((N_HEADS, LANES, TQ), BF16),
            pltpu.SMEM((N_HEADS,), F32),
            pltpu.VMEM((WIDTH, TQ), F32),
        ],
        compiler_params=_cparams(("arbitrary", "arbitrary")),
        name="sb_prompt",
    )(qb, kb, vb)


def _tile_rows(x, n):
    return jnp.concatenate([x] * n, axis=0)


def _pick_heads(o, n_new):
    head_of_col = lax.broadcasted_iota(I32, (n_new, WIDTH), 1) // HEAD_DIM
    out = jnp.zeros((n_new, WIDTH), F32)
    for h in range(N_HEADS):
        out = jnp.where(head_of_col == h, o[h * n_new:(h + 1) * n_new], out)
    return out


def _idx_sample_kernel(pt_ref, qi_ref, wi_ref, *refs, n_pages, top_k, n_new, n_pos_bits):
    page_refs = refs[:n_pages]
    kn_ref, bias_ref, keys_sc = refs[n_pages:]
    lane = lax.broadcasted_iota(I32, (n_new, LANES), 1)
    qrow = lax.broadcasted_iota(I32, (n_new, LANES), 0)

    def page_scores(k_page):
        r = lax.dot_general(qi_ref[0], k_page, (((1,), (1,)), ((), ())),
                            preferred_element_type=F32)
        rel = jnp.maximum(r, 0.0) * wi_ref[0]
        sc = jnp.zeros((n_new, LANES), F32)
        for h in range(N_HEADS):
            sc = sc + rel[h * n_new:(h + 1) * n_new]
        return _sortable_key(sc)

    for p in range(n_pages):
        keys_sc[p] = page_scores(page_refs[p][0].astype(BF16))
    keys_sc[n_pages] = jnp.where(lane <= qrow, page_scores(kn_ref[0]), INT_MIN)

    def count_fn(pred):
        def body(pp, acc):
            return acc + pred(keys_sc[pp], pp * PAGE_SIZE + lane).astype(I32)
        acc = lax.fori_loop(0, n_pages + 1, body, jnp.zeros((n_new, LANES), I32))
        return jnp.sum(acc, axis=1, keepdims=True)

    thr, tie_pos = _topk_select(count_fn, (n_new, 1), top_k, n_pos_bits)
    for p in range(n_pages + 1):
        key = keys_sc[p]
        sel = (key > thr) | ((key == thr) & ((p * PAGE_SIZE + lane) <= tie_pos))
        bias_ref[0, :, p * LANES:(p + 1) * LANES] = jnp.where(sel, 0.0, NEG_BIG).astype(F32)


def _att_sample_kernel(pt_ref, q_ref, bias_ref, bias_new_ref, *refs, pps, n_steps, n_new):
    k_refs, v_refs = refs[:pps], refs[pps:2 * pps]
    kn_ref, vn_ref, o_ref, kbuf, vbuf, m_sc, l_sc, acc_sc = refs[2 * pps:]
    c = pl.program_id(1)

    @pl.when(c == 0)
    def _():
        m_sc[...] = jnp.full(m_sc.shape, NEG_BIG, F32)
        l_sc[...] = jnp.zeros(l_sc.shape, F32)
        acc_sc[...] = jnp.zeros(acc_sc.shape, F32)

    def process(k_rows, v_rows, bias):
        s = lax.dot_general(q_ref[0], k_rows, (((1,), (1,)), ((), ())),
                            preferred_element_type=F32)
        s = s + _tile_rows(bias, N_HEADS)
        m_old = m_sc[...]
        m_new = jnp.maximum(m_old, jnp.max(s, axis=1, keepdims=True))
        pr = jnp.exp(s - m_new)
        alpha = jnp.exp(m_old - m_new)
        l_sc[...] = alpha * l_sc[...] + jnp.sum(pr, axis=1, keepdims=True)
        acc_sc[...] = alpha * acc_sc[...] + jnp.dot(pr.astype(BF16), v_rows, preferred_element_type=F32)
        m_sc[...] = m_new

    for p in range(pps):
        kbuf[p * PAGE_SIZE:(p + 1) * PAGE_SIZE, :] = k_refs[p][0].astype(BF16)
        vbuf[p * PAGE_SIZE:(p + 1) * PAGE_SIZE, :] = v_refs[p][0].astype(BF16)
    process(kbuf[...], vbuf[...], bias_ref[0])

    @pl.when(c == n_steps - 1)
    def _():
        process(kn_ref[0], vn_ref[0], bias_new_ref[0])
        o_ref[0] = _pick_heads(acc_sc[...] / l_sc[...], n_new)


def _sb_sample_kernel(pt_ref, q_ref, *refs, pps, n_steps, n_new):
    k_refs, v_refs = refs[:pps], refs[pps:2 * pps]
    kn_ref, vn_ref, o_ref, kbuf, vbuf, carry_sc, acc_sc = refs[2 * pps:]
    c = pl.program_id(1)
    rows = N_HEADS * n_new
    r_i = lax.broadcasted_iota(I32, (LANES, LANES), 0)
    c_i = lax.broadcasted_iota(I32, (LANES, LANES), 1)
    tri = jnp.where(r_i >= c_i, 1.0, 0.0).astype(BF16)

    def page_weights(z, lk, carry):
        hi = lk.astype(BF16)
        lo = (lk - hi.astype(F32)).astype(BF16)
        suffix = (jnp.dot(hi, tri, preferred_element_type=F32)
                  + jnp.dot(lo, tri, preferred_element_type=F32))
        return jnp.exp(z + suffix + carry), carry + jnp.sum(lk, axis=1, keepdims=True)

    @pl.when(c == 0)
    def _():
        lane = lax.broadcasted_iota(I32, (rows, LANES), 1)
        mask = lane < lax.broadcasted_iota(I32, (rows, LANES), 0) % n_new
        z = lax.dot_general(q_ref[0], kn_ref[0], (((1,), (1,)), ((), ())), preferred_element_type=F32)
        lk = jnp.where(mask, _neg_softplus(z), 0.0)
        a, carry = page_weights(z, lk, jnp.zeros((rows, 1), F32))
        a = jnp.where(mask, a, 0.0)
        acc_sc[...] = jnp.dot(a.astype(BF16), vn_ref[0], preferred_element_type=F32)
        carry_sc[...] = carry

    for p in range(pps):
        kbuf[p * PAGE_SIZE:(p + 1) * PAGE_SIZE, :] = k_refs[p][0].astype(BF16)
        vbuf[p * PAGE_SIZE:(p + 1) * PAGE_SIZE, :] = v_refs[p][0].astype(BF16)
    z = lax.dot_general(q_ref[0], kbuf[...], (((1,), (1,)), ((), ())), preferred_element_type=F32)
    lk = _neg_softplus(z)
    carry = carry_sc[...]
    parts = [None] * pps
    for p in reversed(range(pps)):
        cols = slice(p * PAGE_SIZE, (p + 1) * PAGE_SIZE)
        parts[p], carry = page_weights(z[:, cols], lk[:, cols], carry)
    a = jnp.concatenate(parts, axis=1)
    acc_sc[...] += jnp.dot(a.astype(BF16), vbuf[...], preferred_element_type=F32)
    carry_sc[...] = carry

    @pl.when(c == n_steps - 1)
    def _():
        o_ref[0] = _pick_heads(acc_sc[...], n_new)


def _sample_attention(page_table, qi_rows, wi_rows, qa_bd, qb_bd, new_pages, caches, top_k):
    db, n_pages = page_table.shape
    n_new = qi_rows.shape[1] // N_HEADS
    rows = N_HEADS * n_new
    pt_flat = page_table.reshape(-1)
    n_pos_bits = int(n_pages * PAGE_SIZE + PAGE_SIZE).bit_length()
    pps = min(PAGES_PER_STEP, n_pages)
    n_steps = n_pages // pps
    assert n_steps * pps == n_pages
    per_b = lambda b, *_: (b, 0, 0)

    def page_spec(width, page_of):
        return pl.BlockSpec((1, PAGE_SIZE, width), lambda b, *a: (a[-1][b * n_pages + page_of(*a[:-1])], 0, 0))

    idx_pages = [page_spec(IDX_DIM, lambda p=p: p) for p in range(n_pages)]
    bias = pl.pallas_call(
        functools.partial(_idx_sample_kernel, n_pages=n_pages, top_k=top_k, n_new=n_new,
                          n_pos_bits=n_pos_bits),
        grid_spec=pltpu.PrefetchScalarGridSpec(
            num_scalar_prefetch=1,
            grid=(db,),
            in_specs=[pl.BlockSpec((1, rows, IDX_DIM), per_b), pl.BlockSpec((1, rows, LANES), per_b)]
                     + idx_pages + [pl.BlockSpec((1, PAGE_SIZE, IDX_DIM), per_b)],
            out_specs=pl.BlockSpec((1, n_new, (n_pages + 1) * LANES), per_b),
            scratch_shapes=[pltpu.VMEM((n_pages + 1, n_new, LANES), I32)],
        ),
        out_shape=jax.ShapeDtypeStruct((db, n_new, (n_pages + 1) * LANES), F32),
        compiler_params=_cparams(("arbitrary",)),
        name="idx_sample",
    )(pt_flat, qi_rows, wi_rows, *([caches["k_idx"]] * n_pages), new_pages["k_idx"])

    fwd = [page_spec(WIDTH, lambda c, p=p: c * pps + p) for p in range(pps)]
    rev = [page_spec(WIDTH, lambda c, p=p: n_pages - pps * (c + 1) + p) for p in range(pps)]
    new_spec = pl.BlockSpec((1, PAGE_SIZE, WIDTH), per_b)
    q_spec = pl.BlockSpec((1, rows, WIDTH), per_b)
    o_spec = pl.BlockSpec((1, n_new, WIDTH), per_b)
    page_bufs = [pltpu.VMEM((pps * PAGE_SIZE, WIDTH), BF16)] * 2
    oa = pl.pallas_call(
        functools.partial(_att_sample_kernel, pps=pps, n_steps=n_steps, n_new=n_new),
        grid_spec=pltpu.PrefetchScalarGridSpec(
            num_scalar_prefetch=1,
            grid=(db, n_steps),
            in_specs=[q_spec,
                      pl.BlockSpec((1, n_new, pps * LANES), lambda b, c, pt: (b, 0, c)),
                      pl.BlockSpec((1, n_new, LANES), lambda b, c, pt: (b, 0, n_pages))]
                     + fwd + fwd + [new_spec, new_spec],
            out_specs=o_spec,
            scratch_shapes=page_bufs + [pltpu.VMEM((rows, 1), F32), pltpu.VMEM((rows, 1), F32),
                                        pltpu.VMEM((rows, WIDTH), F32)],
        ),
        out_shape=jax.ShapeDtypeStruct((db, n_new, WIDTH), F32),
        compiler_params=_cparams(("arbitrary", "arbitrary")),
        name="att_sample",
    )(pt_flat, qa_bd, bias, bias, *([caches["k_a"]] * pps), *([caches["v_a"]] * pps),
      new_pages["k_a"], new_pages["v_a"])

    ob = pl.pallas_call(
        functools.partial(_sb_sample_kernel, pps=pps, n_steps=n_steps, n_new=n_new),
        grid_spec=pltpu.PrefetchScalarGridSpec(
            num_scalar_prefetch=1,
            grid=(db, n_steps),
            in_specs=[q_spec] + rev + rev + [new_spec, new_spec],
            out_specs=o_spec,
            scratch_shapes=page_bufs + [pltpu.VMEM((rows, 1), F32), pltpu.VMEM((rows, WIDTH), F32)],
        ),
        out_shape=jax.ShapeDtypeStruct((db, n_new, WIDTH), F32),
        compiler_params=_cparams(("arbitrary", "arbitrary")),
        name="sb_sample",
    )(pt_flat, qb_bd, *([caches["k_b"]] * pps), *([caches["v_b"]] * pps),
      new_pages["k_b"], new_pages["v_b"])
    return oa, ob


def _rms(x, g):
    ms = jnp.mean(x * x, axis=-1, keepdims=True)
    return (x * lax.rsqrt(ms + RMS_EPS)) * g


def _merge_kernel(x_ref, oa_ref, ob_ref, gate_ref, wa_ref, wb_ref, wo_ref, g_ref, x1_ref, h2_ref):
    pa = jnp.dot(oa_ref[0].astype(BF16), wa_ref[...], preferred_element_type=F32)
    pb = jnp.dot(ob_ref[0].astype(BF16), wb_ref[...], preferred_element_type=F32)
    ga = gate_ref[0, :, 0:D_MODEL]
    gb = gate_ref[0, :, D_MODEL:2 * D_MODEL]
    m = jax.nn.sigmoid(ga) * pa + jax.nn.sigmoid(gb) * pb
    x1 = x_ref[0] + jnp.dot(m.astype(BF16), wo_ref[...], preferred_element_type=F32)
    x1_ref[0] = x1
    h2_ref[0] = _rms(x1, g_ref[...]).astype(BF16)


def _merge(x3, oa, ob, gates, wa, wb, wo, g_ffn):
    groups, rows, _ = x3.shape
    blk = lambda n: pl.BlockSpec((1, TR, n), lambda b, t: (b, t, 0))
    const = lambda shape: _resident(shape, lambda b, t: (0,) * len(shape))
    return pl.pallas_call(
        _merge_kernel,
        grid=(groups, rows // TR),
        in_specs=[blk(D_MODEL), blk(WIDTH), blk(WIDTH), blk(2 * D_MODEL),
                  const((WIDTH, D_MODEL)), const((WIDTH, D_MODEL)), const((D_MODEL, D_MODEL)),
                  const((1, D_MODEL))],
        out_specs=[blk(D_MODEL), blk(D_MODEL)],
        out_shape=[jax.ShapeDtypeStruct((groups, rows, D_MODEL), F32),
                   jax.ShapeDtypeStruct((groups, rows, D_MODEL), BF16)],
        compiler_params=_cparams(("arbitrary", "arbitrary")),
        name="merge",
    )(x3, oa, ob, gates, wa, wb, wo, g_ffn)


def _ffn_kernel(*refs, seq_len, slab, has_state):
    if has_state:
        (x1_ref, h2_ref, wup_ref, cw_ref, cb_ref, wdn_ref, gf_ref, s1_ref, s2_ref,
         y_ref, a_ref, buf) = refs
    else:
        x1_ref, h2_ref, wup_ref, cw_ref, cb_ref, wdn_ref, gf_ref, y_ref, a_ref, buf = refs
    t = pl.program_id(1)
    u = jnp.dot(h2_ref[0], wup_ref[...], preferred_element_type=F32)
    a = u[:, :D_FF]
    b = u[:, D_FF:]

    @pl.when(t == 0)
    def _():
        buf[0:SUBLANES, :] = jnp.zeros((SUBLANES, D_FF), F32)

    buf[SUBLANES:SUBLANES + TR, :] = a
    prev1 = buf[SUBLANES - 1:SUBLANES - 1 + TR, :]
    prev2 = buf[SUBLANES - 2:SUBLANES - 2 + TR, :]
    if has_state:
        tok = lax.broadcasted_iota(I32, (TR, 1), 0) % seq_len
        prev1 = jnp.where(tok == 0, s1_ref[...], prev1)
        prev2 = jnp.where(tok <= 1, s2_ref[...], prev2)
    cw = cw_ref[...]
    a_c = cb_ref[...] + cw[0:1, :] * prev2 + cw[1:2, :] * prev1 + cw[2:3, :] * a
    gelu = 0.5 * a_c * (1.0 + lax.erf(a_c * (2.0 ** -0.5)))
    f = jnp.dot((gelu * b).astype(BF16), wdn_ref[...], preferred_element_type=F32)
    y_ref[0] = _rms(x1_ref[0] + f, gf_ref[...])
    a_ref[0] = a[slab[0]:slab[0] + slab[1], :]
    buf[0:SUBLANES, :] = a[TR - SUBLANES:, :]


def _ffn(x1, h2, w_up, conv_w, conv_b, w_down, g_final, slab, state=None, seq_len=0):
    nseq, rows = x1.shape[0], x1.shape[1]
    nt = rows // TR
    blk = lambda n: pl.BlockSpec((1, TR, n), lambda s, t: (s, t, 0))
    const = lambda shape: _resident(shape, lambda s, t: (0,) * len(shape))
    in_specs = [blk(D_MODEL), blk(D_MODEL), const((D_MODEL, 2 * D_FF)), const((3, D_FF)),
                const((1, D_FF)), const((D_FF, D_MODEL)), const((1, D_MODEL))]
    args = [x1, h2, w_up, conv_w, conv_b, w_down, g_final]
    if state is not None:
        in_specs += [const((TR, D_FF)), const((TR, D_FF))]
        args += list(state)
    return pl.pallas_call(
        functools.partial(_ffn_kernel, seq_len=seq_len, slab=slab, has_state=state is not None),
        grid=(nseq, nt),
        in_specs=in_specs,
        out_specs=[blk(D_MODEL), pl.BlockSpec((1, slab[1], D_FF), lambda s, t: (s, t, 0))],
        out_shape=[jax.ShapeDtypeStruct((nseq, rows, D_MODEL), F32),
                   jax.ShapeDtypeStruct((nseq, nt * slab[1], D_FF), F32)],
        scratch_shapes=[pltpu.VMEM((SUBLANES + TR, D_FF), F32)],
        compiler_params=_cparams(("arbitrary", "arbitrary")),
        name="conv_ffn",
    )(*args)


def _rope_tables(positions):
    half = HEAD_DIM // 2
    inv = ROPE_THETA ** (-np.arange(half, dtype=np.float64) * 2.0 / HEAD_DIM)
    ang = np.asarray(positions, np.float64)[:, None] * inv[None, :]
    cos = np.tile(np.cos(ang), (1, LANES // half))
    sin = np.sin(ang)
    sin = np.tile(np.concatenate([-sin, sin], axis=1), (1, LANES // HEAD_DIM))
    return cos.astype(np.float32), sin.astype(np.float32)


def _permute_w_in(w):
    off = np.cumsum([0, WIDTH, WIDTH, WIDTH, WIDTH, IDX_DIM, N_HEADS, WIDTH, WIDTH, WIDTH,
                     D_MODEL, D_MODEL])
    qa, ka, va, qi, ki, wi, qb, kb, vb, ga, gb = [w[:, off[j]:off[j + 1]] for j in range(11)]
    pad = jnp.zeros((w.shape[0], C_VA - C_WI - N_HEADS), w.dtype)
    out = jnp.concatenate([ga, gb, qa, ka, qi, ki, wi, pad, va, qb, kb, vb], axis=1)
    assert out.shape[1] == NW
    return out.astype(BF16)


def _block_diag_rows(q):
    db, n_new, _ = q.shape
    head_of_col = jnp.arange(WIDTH) // HEAD_DIM
    keep = head_of_col[None, :] == jnp.arange(N_HEADS)[:, None]
    out = jnp.where(keep[None, :, None, :], q[:, None, :, :], 0)
    return out.reshape(db, N_HEADS * n_new, WIDTH)


def _pad_page(x):
    return jnp.pad(x, ((0, 0), (0, PAGE_SIZE - x.shape[1]), (0, 0)))


def kernel(x_prompt, x_sample, cache_k_a, cache_v_a, cache_k_idx, cache_k_b, cache_v_b, state_conv,
           page_table, meta_tokens, g_attn, w_in, w_branch_a, w_branch_b, w_o, g_ffn, w_up, conv_w,
           conv_b, w_down, g_final):
    batch, seq = x_prompt.shape[0], x_prompt.shape[1]
    db, n_new = x_sample.shape[0], x_sample.shape[1]
    n_pages = page_table.shape[1]
    past_len = n_pages * PAGE_SIZE
    s_len = N_META + seq
    s_pad = -(-s_len // TQ) * TQ
    top_k_p = min(TOP_K_MAX, seq // 4)
    top_k_s = min(TOP_K_MAX, (past_len + n_new) // 4)
    assert db * n_new == TR and cache_k_a.shape[0] == 1
    assert (s_len - 2) // SUBLANES == (s_len - 1) // SUBLANES

    meta = jnp.broadcast_to(meta_tokens[None], (batch, N_META, D_MODEL))
    xp = jnp.concatenate([meta, x_prompt, jnp.zeros((batch, s_pad - s_len, D_MODEL), F32)], axis=1)
    xs = x_sample.reshape(1, TR, D_MODEL)
    w_perm = _permute_w_in(w_in[0])
    cos_p, sin_p = _rope_tables(np.arange(s_pad))
    cos_s, sin_s = _rope_tables(past_len + np.arange(TR) % n_new)

    (ka_p, va_p, ki_p, kb_p, vb_p, qa_pb, qi_pb, qb_pb, ka_pb, va_pb, kb_pb, vb_pb, ki_pb, kiwi_p,
     gate_p) = _in_projection(xp, g_attn, w_perm, jnp.asarray(cos_p), jnp.asarray(sin_p),
                              lambda b, t: t, s_len)
    (ka_s, va_s, ki_s, kb_s, vb_s, qa_sb, qi_sb, qb_sb, ka_sb, va_sb, kb_sb, vb_sb, ki_sb, kiwi_s,
     gate_s) = _in_projection(xs, g_attn, w_perm, jnp.asarray(cos_s), jnp.asarray(sin_s),
                              lambda b, t: t, TR)

    oa_p = _dsa_prompt(qa_pb, qi_pb, kiwi_p, ka_pb, ki_pb, va_pb, top_k_p)
    ob_p = _sb_prompt(qb_pb, kb_pb, vb_pb)

    per_seq = lambda a: a.reshape(db, n_new, a.shape[-1])
    qi_rows = per_seq(qi_sb).reshape(db, n_new, N_HEADS, IDX_DIM).transpose(0, 2, 1, 3)
    qi_rows = qi_rows.reshape(db, N_HEADS * n_new, IDX_DIM)
    wi_s = per_seq(kiwi_s)[..., C_WI - C_KI:C_WI - C_KI + N_HEADS]
    wi_rows = jnp.broadcast_to(wi_s.transpose(0, 2, 1).reshape(db, N_HEADS * n_new, 1),
                               (db, N_HEADS * n_new, LANES))
    n_phys = cache_k_a.shape[1]
    caches = {
        "k_a": cache_k_a.reshape(n_phys, PAGE_SIZE, WIDTH),
        "v_a": cache_v_a.reshape(n_phys, PAGE_SIZE, WIDTH),
        "k_idx": cache_k_idx.reshape(n_phys, PAGE_SIZE, IDX_DIM),
        "k_b": cache_k_b.reshape(n_phys, PAGE_SIZE, WIDTH),
        "v_b": cache_v_b.reshape(n_phys, PAGE_SIZE, WIDTH),
    }
    new_pages = {name: _pad_page(per_seq(arr)) for name, arr in
                 (("k_a", ka_sb), ("v_a", va_sb), ("k_idx", ki_sb), ("k_b", kb_sb), ("v_b", vb_sb))}
    oa_s, ob_s = _sample_attention(page_table, qi_rows, wi_rows, _block_diag_rows(per_seq(qa_sb)),
                                   _block_diag_rows(per_seq(qb_sb)), new_pages, caches, top_k_s)

    merge_w = (w_branch_a[0].astype(BF16), w_branch_b[0].astype(BF16), w_o[0].astype(BF16), g_ffn)
    x1_p, h2_p = _merge(xp, oa_p, ob_p, gate_p, *merge_w)
    x1_s, h2_s = _merge(xs, oa_s.reshape(1, TR, WIDTH), ob_s.reshape(1, TR, WIDTH), gate_s, *merge_w)
    ffn_w = (w_up[0].astype(BF16), conv_w[0], conv_b, w_down[0].astype(BF16), g_final[None])
    tail_tile, tail_off = (s_len - 2) // TR, (s_len - 2) % TR
    slab_off = tail_off // SUBLANES * SUBLANES
    y_p, a_p = _ffn(x1_p, h2_p, *ffn_w, slab=(slab_off, SUBLANES))
    st = state_conv[0]
    zeros = jnp.zeros((db, n_new - 1, D_FF), F32)
    s1 = jnp.concatenate([st[:, 1:2], zeros], axis=1).reshape(TR, D_FF)
    s2 = jnp.concatenate([st[:, 0:1], st[:, 1:2], zeros[:, 1:]], axis=1).reshape(TR, D_FF)
    y_s, a_s = _ffn(x1_s, h2_s, *ffn_w, slab=(0, TR), state=(s1, s2), seq_len=n_new)

    y_prompt = y_p[:, N_META:s_len]
    y_sample = y_s.reshape(db, n_new, D_MODEL)
    a_tail = tail_tile * SUBLANES + tail_off - slab_off
    conv_p = a_p[:, a_tail:a_tail + 2][None]
    conv_s = a_s.reshape(db, n_new, D_FF)[:, n_new - 2:][None]
    heads_p = lambda a: a.reshape(1, batch, s_len, N_HEADS, HEAD_DIM)
    heads_s = lambda a: a.reshape(1, db, n_new, N_HEADS, HEAD_DIM)
    return (y_prompt, y_sample,
            heads_p(ka_p), heads_p(va_p), ki_p[None], heads_p(kb_p), heads_p(vb_p), conv_p,
            heads_s(ka_s), heads_s(va_s), ki_s.reshape(1, db, n_new, IDX_DIM),
            heads_s(kb_s), heads_s(vb_s), conv_s)
```

```python
import functools

import numpy as np
import jax
import jax.numpy as jnp
from jax import lax
from jax.experimental import pallas as pl
from jax.experimental.pallas import tpu as pltpu

F32 = jnp.float32
BF16 = jnp.bfloat16
I32 = jnp.int32

D_MODEL = 1024
N_HEADS = 8
HEAD_DIM = 64
WIDTH = N_HEADS * HEAD_DIM
IDX_DIM = 64
N_META = 16
PAGE_SIZE = 128
TOP_K_MAX = 256
ROPE_THETA = 10000.0
D_FF = 2816
RMS_EPS = 1e-6

LANES = 128
SUBLANES = 8
TQ = 256
TK = 256
TR = 256
RUN = TK // SUBLANES
PAGES_PER_STEP = 16
PAGES_PER_STEP_STORED = 8
VMEM_LIMIT = 56 * 1024 * 1024

INT_MIN = -2 ** 31
NEG_BIG = -1e30
EXP_ZERO = -104.0

C_GA, C_GB = 0, 1024
C_QA, C_KA, C_QI = 2048, 2560, 3072
C_KI, C_WI = 3584, 3648
C_VA, C_QB, C_KB, C_VB = 3712, 4224, 4736, 5248
NW = 5760
ROPE_LO, ROPE_HI = C_QA, C_VA


def _cparams(sem):
    return pltpu.CompilerParams(dimension_semantics=sem, vmem_limit_bytes=VMEM_LIMIT)


def _resident(shape, index_map):
    return pl.BlockSpec(shape, index_map, pipeline_mode=pl.Buffered(1))


def _inproj_kernel(x_ref, g_ref, w_ref, cos_ref, sin_ref,
                   ka_f, va_f, ki_f, kb_f, vb_f,
                   qa_b, qi_b, qb_b, ka_b, va_b, kb_b, vb_b, ki_b, kiwi_f, gate_f):
    x = x_ref[0]
    ms = jnp.mean(x * x, axis=-1, keepdims=True)
    h = (x * lax.rsqrt(ms + RMS_EPS)) * g_ref[...]
    y = jnp.dot(h.astype(BF16), w_ref[...], preferred_element_type=F32)
    cos = cos_ref[...]
    sin = sin_ref[...]
    lane = lax.broadcasted_iota(I32, (1, LANES), 1)
    first_half = (lane % HEAD_DIM) < (HEAD_DIM // 2)
    q_scale = HEAD_DIM ** -0.5
    w_lanes = (lane >= C_WI - C_KI) & (lane < C_WI - C_KI + N_HEADS)

    def rope(yc):
        partner = jnp.where(first_half, pltpu.roll(yc, LANES - HEAD_DIM // 2, 1),
                            pltpu.roll(yc, HEAD_DIM // 2, 1))
        return yc * cos + partner * sin

    plan = [(C_GA, 2 * D_MODEL, False, 1.0, gate_f, None),
            (C_QA, WIDTH, True, q_scale, None, qa_b), (C_KA, WIDTH, True, 1.0, ka_f, ka_b),
            (C_QI, WIDTH, True, q_scale, None, qi_b),
            (C_VA, WIDTH, False, 1.0, va_f, va_b), (C_QB, WIDTH, False, q_scale, None, qb_b),
            (C_KB, WIDTH, False, 1.0, kb_f, kb_b), (C_VB, WIDTH, False, 1.0, vb_f, vb_b)]
    for c_lo, width, rotary, scale, dst_f, dst_b in plan:
        for off in range(0, width, LANES):
            r = y[:, c_lo + off:c_lo + off + LANES]
            if rotary:
                r = rope(r)
            if scale != 1.0:
                r = r * scale
            if dst_f is not None:
                dst_f[0, :, off:off + LANES] = r
            if dst_b is not None:
                dst_b[0, :, off:off + LANES] = r.astype(BF16)
    yc = y[:, C_KI:C_KI + LANES]
    r = jnp.where(lane < IDX_DIM, rope(yc), yc * jnp.where(w_lanes, N_HEADS ** -0.5, 1.0).astype(F32))
    kiwi_f[0] = r
    ki_f[0] = r[:, :IDX_DIM]
    ki_b[0] = r[:, :IDX_DIM].astype(BF16)


def _in_projection(x3, g, w_perm, cos_tab, sin_tab, tab_index, leaf_len):
    groups, rows, _ = x3.shape
    blk = lambda n: pl.BlockSpec((1, TR, n), lambda b, t: (b, t, 0))
    leaf = lambda n: jax.ShapeDtypeStruct((groups, leaf_len, n), F32)
    full = lambda n, dt: jax.ShapeDtypeStruct((groups, rows, n), dt)
    tab_spec = pl.BlockSpec((TR, LANES), lambda b, t: (tab_index(b, t), 0))
    return pl.pallas_call(
        _inproj_kernel,
        grid=(groups, rows // TR),
        in_specs=[blk(D_MODEL), _resident((1, D_MODEL), lambda b, t: (0, 0)),
                  _resident((D_MODEL, NW), lambda b, t: (0, 0)), tab_spec, tab_spec],
        out_specs=[blk(WIDTH), blk(WIDTH), blk(IDX_DIM), blk(WIDTH), blk(WIDTH)]
                  + [blk(WIDTH)] * 7 + [blk(IDX_DIM), blk(LANES), blk(2 * D_MODEL)],
        out_shape=[leaf(WIDTH), leaf(WIDTH), leaf(IDX_DIM), leaf(WIDTH), leaf(WIDTH)]
                  + [full(WIDTH, BF16)] * 7 + [full(IDX_DIM, BF16), full(LANES, F32),
                                               full(2 * D_MODEL, F32)],
        compiler_params=_cparams(("arbitrary", "arbitrary")),
        name="in_projection",
    )(x3, g, w_perm, cos_tab, sin_tab)


def _sortable_key(score):
    bits = lax.bitcast_convert_type(score, I32)
    return bits ^ ((bits >> 31) & jnp.int32(0x7FFFFFFF))


def _topk_select(count_fn, shape, top_k, n_pos_bits):
    def bit_step(it, thr):
        cand = thr + (jnp.int32(1) << (31 - it))
        cnt = count_fn(lambda key, kpos: key >= cand)
        return jnp.where(cnt >= top_k, cand, thr)

    thr = lax.fori_loop(0, 32, bit_step, jnp.full(shape, INT_MIN, I32))
    c_ge = count_fn(lambda key, kpos: key >= thr)
    c_gt = count_fn(lambda key, kpos: key > thr)
    need = top_k - c_gt
    has_tie = (c_ge > top_k) & (thr > INT_MIN)

    def pos_step(it, x):
        cand = x + (jnp.int32(1) << (n_pos_bits - 1 - it))
        cnt = count_fn(lambda key, kpos: (key == thr) & (kpos < cand))
        return jnp.where(cnt < need, cand, x)

    def tie_search():
        return lax.fori_loop(0, n_pos_bits, pos_step, jnp.zeros(shape, I32))

    any_tie = jnp.max(has_tie.astype(I32)) > 0
    tie_x = lax.cond(any_tie, tie_search, lambda: jnp.zeros(shape, I32))
    big = jnp.int32(2 ** 30)
    tie_pos = jnp.where(thr == INT_MIN, -1, jnp.where(has_tie, tie_x, big))
    return thr, tie_pos


def _masked_pairs(q_t, qm_sc):
    zeros = jnp.zeros((HEAD_DIM, q_t.shape[1]), F32)
    for h in range(N_HEADS):
        part = q_t[h * HEAD_DIM:(h + 1) * HEAD_DIM]
        pair = [part, zeros] if h % 2 == 0 else [zeros, part]
        qm_sc[h] = jnp.concatenate(pair, axis=0).astype(BF16)


def _dsa_prompt_kernel(qa_ref, qi_ref, kiwi_ref, ka_ref, ki_ref, va_ref, o_ref,
                       keys_sc, vt_sc, qm_sc, qit_sc, m_sc, l_sc, acc_sc, s_sc,
                       *, top_k, n_pos_bits, n_key_blocks):
    i = pl.program_id(1)
    nkb = i + 1
    row = lax.broadcasted_iota(I32, (TK, TQ), 0)
    col = lax.broadcasted_iota(I32, (TK, TQ), 1)

    @pl.when(i == 0)
    def _():
        def transpose_block(kb, carry):
            blk = va_ref[0, pl.ds(pl.multiple_of(kb * TK, TK), TK), :]
            vt_sc[kb] = blk.astype(F32).T.astype(BF16)
            return carry
        lax.fori_loop(0, n_key_blocks, transpose_block, 0)

    _masked_pairs(qa_ref[0].astype(F32).T, qm_sc)
    qit_sc[...] = qi_ref[0].astype(F32).T.astype(BF16)
    wit = kiwi_ref[0].T[C_WI - C_KI:C_WI - C_KI + N_HEADS, :]

    def score_block(kb, carry):
        k_idx = ki_ref[0, pl.ds(pl.multiple_of(kb * TK, TK), TK), :]
        sc = jnp.zeros((TK, TQ), F32)
        for h in range(N_HEADS):
            r = jnp.dot(k_idx, qit_sc[h * IDX_DIM:(h + 1) * IDX_DIM, :], preferred_element_type=F32)
            sc = sc + wit[h:h + 1, :] * jnp.maximum(r, 0.0)
        keys_sc[kb] = _sortable_key(sc)
        return carry

    lax.fori_loop(0, nkb, score_block, 0)
    keys_sc[i] = jnp.where(row <= col, keys_sc[i], INT_MIN)

    def count_fn(pred):
        def body(kb, acc):
            m = pred(keys_sc[kb], kb * TK + row).astype(I32)
            for g in range(TK // SUBLANES):
                acc = acc + m[g * SUBLANES:(g + 1) * SUBLANES]
            return acc
        acc = lax.fori_loop(0, nkb, body, jnp.zeros((SUBLANES, TQ), I32))
        return jnp.sum(acc, axis=0, keepdims=True)

    thr, tie_pos = _topk_select(count_fn, (1, TQ), top_k, n_pos_bits)

    m_sc[...] = jnp.full(m_sc.shape, NEG_BIG, F32)
    l_sc[...] = jnp.zeros(l_sc.shape, F32)
    acc_sc[...] = jnp.zeros(acc_sc.shape, F32)

    def attend_block(kb, carry):
        key = keys_sc[kb]
        sel = (key > thr) | ((key == thr) & ((kb * TK + row) <= tie_pos))
        bias = jnp.where(sel, 0.0, NEG_BIG).astype(F32)
        start = pl.multiple_of(kb * TK, TK)

        m_new = []
        for h in range(N_HEADS):
            k_pair = ka_ref[0, pl.ds(start, TK), (h // 2) * LANES:(h // 2 + 1) * LANES]
            s = jnp.dot(k_pair, qm_sc[h], preferred_element_type=F32) + bias
            s_sc[h] = s
            m_new.append(jnp.maximum(m_sc[h:h + 1, :], jnp.max(s, axis=0, keepdims=True)))
        for h in range(N_HEADS):
            p = jnp.exp(s_sc[h] - m_new[h])
            alpha = jnp.exp(m_sc[h:h + 1, :] - m_new[h])
            l_sc[h:h + 1, :] = alpha * l_sc[h:h + 1, :] + jnp.sum(p, axis=0, keepdims=True)
            pv = jnp.dot(vt_sc[kb, h * HEAD_DIM:(h + 1) * HEAD_DIM, :], p.astype(BF16),
                         preferred_element_type=F32)
            rows = slice(h * HEAD_DIM, (h + 1) * HEAD_DIM)
            acc_sc[rows, :] = alpha * acc_sc[rows, :] + pv
            m_sc[h:h + 1, :] = m_new[h]
        return carry

    lax.fori_loop(0, nkb, attend_block, 0)
    for h in range(N_HEADS):
        rows = slice(h * HEAD_DIM, (h + 1) * HEAD_DIM)
        acc_sc[rows, :] = acc_sc[rows, :] / l_sc[h:h + 1, :]
    o_ref[0] = acc_sc[...].T


def _dsa_prompt(qa, qi, kiwi, ka, ki, va, top_k):
    b, s_pad, _ = ka.shape
    nkb = s_pad // TK
    kern = functools.partial(_dsa_prompt_kernel, top_k=top_k, n_pos_bits=int(s_pad).bit_length(),
                             n_key_blocks=nkb)
    q_blk = lambda n: pl.BlockSpec((1, TQ, n), lambda bb, i: (bb, i, 0))
    return pl.pallas_call(
        kern,
        grid=(b, s_pad // TQ),
        in_specs=[q_blk(WIDTH), q_blk(WIDTH), q_blk(LANES),
                  _resident((1, s_pad, WIDTH), lambda bb, i: (bb, 0, 0)),
                  _resident((1, s_pad, IDX_DIM), lambda bb, i: (bb, 0, 0)),
                  _resident((1, s_pad, WIDTH), lambda bb, i: (bb, 0, 0))],
        out_specs=q_blk(WIDTH),
        out_shape=jax.ShapeDtypeStruct((b, s_pad, WIDTH), F32),
        scratch_shapes=[
            pltpu.VMEM((nkb, TK, TQ), I32),
            pltpu.VMEM((nkb, WIDTH, TK), BF16),
            pltpu.VMEM((N_HEADS, LANES, TQ), BF16),
            pltpu.VMEM((WIDTH, TQ), BF16),
            pltpu.VMEM((N_HEADS, TQ), F32),
            pltpu.VMEM((N_HEADS, TQ), F32),
            pltpu.VMEM((WIDTH, TQ), F32),
            pltpu.VMEM((N_HEADS, TK, TQ), F32),
        ],
        compiler_params=_cparams(("arbitrary", "arbitrary")),
        name="dsa_prompt",
    )(qa, qi, kiwi, ka, ki, va)


def _neg_softplus(z):
    return -(jnp.maximum(z, 0.0) + jnp.log(1.0 + jnp.exp(-jnp.abs(z))))


def _sb_prompt_kernel(qb_ref, kb_ref, vb_ref, o_ref,
                      kperm_sc, vt_sc, stage_sc, qm_sc, kinf_sm, acc_sc, *, n_key_blocks):
    i = pl.program_id(1)
    row = lax.broadcasted_iota(I32, (TK, TQ), 0)
    col = lax.broadcasted_iota(I32, (TK, TQ), 1)
    key_off = RUN * (row % SUBLANES) + row // SUBLANES
    strict = key_off < col
    sub = lax.broadcasted_iota(I32, (SUBLANES, TQ), 0)

    def permuted(block_f32):
        cols = []
        for c in range(WIDTH // LANES):
            stage_sc[c] = block_f32[:, c * LANES:(c + 1) * LANES]
            cols.append(jnp.concatenate(
                [stage_sc[c, pl.ds(v, SUBLANES, stride=RUN), :] for v in range(RUN)], axis=0))
        return jnp.concatenate(cols, axis=1)

    @pl.when(i == 0)
    def _():
        def prep(kb, kmax):
            start = pl.multiple_of(kb * TK, TK)
            k_blk = kb_ref[0, pl.ds(start, TK), :].astype(F32)
            k_abs = jnp.abs(k_blk)
            for g in range(TK // SUBLANES):
                kmax = jnp.maximum(kmax, k_abs[g * SUBLANES:(g + 1) * SUBLANES])
            kperm_sc[pl.ds(start, TK), :] = permuted(k_blk).astype(BF16)
            vt_sc[kb] = permuted(vb_ref[0, pl.ds(start, TK), :].astype(F32)).T.astype(BF16)
            return kmax
        kmax = lax.fori_loop(0, n_key_blocks, prep, jnp.zeros((SUBLANES, WIDTH), F32))
        col_max = jnp.max(kmax, axis=0, keepdims=True)
        for h in range(N_HEADS):
            kinf_sm[h] = jnp.max(col_max[:, h * HEAD_DIM:(h + 1) * HEAD_DIM])

    q_t = qb_ref[0].astype(F32).T
    _masked_pairs(q_t, qm_sc)

    def block(z, lk, carry):
        run_sums = []
        acc = jnp.zeros((SUBLANES, TQ), F32)
        for v in reversed(range(RUN)):
            acc = acc + lk[v * SUBLANES:(v + 1) * SUBLANES]
            run_sums.append(acc)
        run_sums = run_sums[::-1]
        tot = run_sums[0]
        later = jnp.zeros((SUBLANES, TQ), F32)
        for k in range(1, SUBLANES):
            shifted = pltpu.roll(tot, SUBLANES - k, 0)
            later = later + jnp.where(sub + k < SUBLANES, shifted, 0.0)
        offs = later + carry
        parts = [jnp.exp(z[v * SUBLANES:(v + 1) * SUBLANES] + run_sums[v] + offs) for v in range(RUN)]
        return jnp.concatenate(parts, axis=0), carry + jnp.sum(tot, axis=0, keepdims=True)

    for pair in range(N_HEADS // 2):
        heads = (2 * pair, 2 * pair + 1)
        lanes = slice(pair * LANES, (pair + 1) * LANES)
        bounds = [jnp.sum(jnp.abs(q_t[h * HEAD_DIM:(h + 1) * HEAD_DIM]), axis=0, keepdims=True)
                  * (kinf_sm[h] * (1.0 + 2.0 ** -8)) for h in heads]

        def live(carries):
            worst = jnp.maximum(carries[0] + bounds[0], carries[1] + bounds[1])
            return jnp.max(worst) > EXP_ZERO

        start = pl.multiple_of(i * TK, TK)
        k_pair = kperm_sc[pl.ds(start, TK), lanes]
        carries = []
        for h in heads:
            rows = slice(h * HEAD_DIM, (h + 1) * HEAD_DIM)
            z = jnp.dot(k_pair, qm_sc[h], preferred_element_type=F32)
            lk = jnp.where(strict, _neg_softplus(z), 0.0)
            a, carry = block(z, lk, jnp.zeros((1, TQ), F32))
            a = jnp.where(strict, a, 0.0)
            acc_sc[rows, :] = jnp.dot(vt_sc[i, rows, :], a.astype(BF16), preferred_element_type=F32)
            carries.append(carry)

        def cond(state):
            kb, _, _, go = state
            return (kb >= 0) & go

        def earlier(state):
            kb, c0, c1, _ = state
            st = pl.multiple_of(kb * TK, TK)
            kp = kperm_sc[pl.ds(st, TK), lanes]
            new = []
            for h, carry in zip(heads, (c0, c1)):
                rows = slice(h * HEAD_DIM, (h + 1) * HEAD_DIM)
                zz = jnp.dot(kp, qm_sc[h], preferred_element_type=F32)
                aa, carry = block(zz, _neg_softplus(zz), carry)
                acc_sc[rows, :] += jnp.dot(vt_sc[kb, rows, :], aa.astype(BF16),
                                           preferred_element_type=F32)
                new.append(carry)
            return kb - 1, new[0], new[1], live(new)

        lax.while_loop(cond, earlier, (i - 1, carries[0], carries[1], live(carries)))
    o_ref[0] = acc_sc[...].T


def _sb_prompt(qb, kb, vb):
    b, s_pad, _ = kb.shape
    nkb = s_pad // TK
    return pl.pallas_call(
        functools.partial(_sb_prompt_kernel, n_key_blocks=nkb),
        grid=(b, s_pad // TQ),
        in_specs=[pl.BlockSpec((1, TQ, WIDTH), lambda bb, i: (bb, i, 0)),
                  _resident((1, s_pad, WIDTH), lambda bb, i: (bb, 0, 0)),
                  _resident((1, s_pad, WIDTH), lambda bb, i: (bb, 0, 0))],
        out_specs=pl.BlockSpec((1, TQ, WIDTH), lambda bb, i: (bb, i, 0)),
        out_shape=jax.ShapeDtypeStruct((b, s_pad, WIDTH), F32),
        scratch_shapes=[
            pltpu.VMEM((s_pad, WIDTH), BF16),
            pltpu.VMEM((nkb, WIDTH, TK), BF16),
            pltpu.VMEM((WIDTH // LANES, TK, LANES), F32),
            pltpu.VMEM((N_HEADS, LANES, TQ), BF16),
            pltpu.SMEM((N_HEADS,), F32),
            pltpu.VMEM((WIDTH, TQ), F32),
        ],
        compiler_params=_cparams(("arbitrary", "arbitrary")),
        name="sb_prompt",
    )(qb, kb, vb)


def _tile_rows(x, n):
    return jnp.concatenate([x] * n, axis=0)


def _pick_heads(o, n_new):
    head_of_col = lax.broadcasted_iota(I32, (n_new, WIDTH), 1) // HEAD_DIM
    out = jnp.zeros((n_new, WIDTH), F32)
    for h in range(N_HEADS):
        out = jnp.where(head_of_col == h, o[h * n_new:(h + 1) * n_new], out)
    return out


def _idx_sample_kernel(pt_ref, qi_ref, wi_ref, *refs, n_pages, top_k, n_new, n_pos_bits):
    page_refs = refs[:n_pages]
    kn_ref, bias_ref, keys_sc = refs[n_pages:]
    lane = lax.broadcasted_iota(I32, (n_new, LANES), 1)
    qrow = lax.broadcasted_iota(I32, (n_new, LANES), 0)

    def page_scores(k_page):
        r = lax.dot_general(qi_ref[0], k_page, (((1,), (1,)), ((), ())),
                            preferred_element_type=F32)
        rel = jnp.maximum(r, 0.0) * wi_ref[0]
        sc = jnp.zeros((n_new, LANES), F32)
        for h in range(N_HEADS):
            sc = sc + rel[h * n_new:(h + 1) * n_new]
        return _sortable_key(sc)

    for p in range(n_pages):
        keys_sc[p] = page_scores(page_refs[p][0].astype(BF16))
    keys_sc[n_pages] = jnp.where(lane <= qrow, page_scores(kn_ref[0]), INT_MIN)

    def count_fn(pred):
        def body(pp, acc):
            return acc + pred(keys_sc[pp], pp * PAGE_SIZE + lane).astype(I32)
        acc = lax.fori_loop(0, n_pages + 1, body, jnp.zeros((n_new, LANES), I32))
        return jnp.sum(acc, axis=1, keepdims=True)

    thr, tie_pos = _topk_select(count_fn, (n_new, 1), top_k, n_pos_bits)
    for p in range(n_pages + 1):
        key = keys_sc[p]
        sel = (key > thr) | ((key == thr) & ((p * PAGE_SIZE + lane) <= tie_pos))
        bias_ref[0, :, p * LANES:(p + 1) * LANES] = jnp.where(sel, 0.0, NEG_BIG).astype(F32)


def _att_sample_kernel(pt_ref, q_ref, bias_ref, bias_new_ref, *refs, pps, n_steps, n_new):
    k_refs, v_refs = refs[:pps], refs[pps:2 * pps]
    kn_ref, vn_ref, o_ref, kbuf, vbuf, m_sc, l_sc, acc_sc = refs[2 * pps:]
    c = pl.program_id(1)
    rows = N_HEADS * n_new
    page_rows = PAGE_SIZE * N_HEADS
    expand = jnp.where(lax.broadcasted_iota(I32, (PAGE_SIZE, page_rows), 1) // N_HEADS
                       == lax.broadcasted_iota(I32, (PAGE_SIZE, page_rows), 0), 1.0, 0.0).astype(BF16)

    @pl.when(c == 0)
    def _():
        m_sc[...] = jnp.full(m_sc.shape, NEG_BIG, F32)
        l_sc[...] = jnp.zeros(l_sc.shape, F32)
        acc_sc[...] = jnp.zeros(acc_sc.shape, F32)

    def keep_mask(bias, n_pages_here, n_cols):
        sel = jnp.where(bias == 0.0, 1.0, 0.0).astype(BF16)
        spread = [jnp.dot(sel[:, p * PAGE_SIZE:(p + 1) * PAGE_SIZE], expand, preferred_element_type=F32)
                  for p in range(n_pages_here)]
        spread = _tile_rows(jnp.concatenate(spread, axis=1)[:, :n_cols], N_HEADS)
        key_head = lax.broadcasted_iota(I32, (rows, n_cols), 1) % N_HEADS
        q_head = lax.broadcasted_iota(I32, (rows, n_cols), 0) // n_new
        return (spread > 0.5) & (key_head == q_head)

    def process(k_rows, v_rows, keep):
        s = lax.dot_general(q_ref[0], k_rows, (((1,), (1,)), ((), ())),
                            preferred_element_type=F32)
        s = jnp.where(keep, s, NEG_BIG)
        m_old = m_sc[...]
        m_new = jnp.maximum(m_old, jnp.max(s, axis=1, keepdims=True))
        pr = jnp.exp(s - m_new)
        alpha = jnp.exp(m_old - m_new)
        l_sc[...] = alpha * l_sc[...] + jnp.sum(pr, axis=1, keepdims=True)
        acc_sc[...] = alpha * acc_sc[...] + jnp.dot(pr.astype(BF16), v_rows, preferred_element_type=F32)
        m_sc[...] = m_new

    for p in range(pps):
        kbuf[p * page_rows:(p + 1) * page_rows, :] = k_refs[p][0, 0].reshape(page_rows, HEAD_DIM).astype(BF16)
        vbuf[p * page_rows:(p + 1) * page_rows, :] = v_refs[p][0, 0].reshape(page_rows, HEAD_DIM).astype(BF16)
    process(kbuf[...], vbuf[...], keep_mask(bias_ref[0], pps, pps * page_rows))

    @pl.when(c == n_steps - 1)
    def _():
        process(kn_ref[0], vn_ref[0], keep_mask(bias_new_ref[0], 1, n_new * N_HEADS))
        o_ref[0] = acc_sc[...] / l_sc[...]


def _sb_sample_kernel(pt_ref, q_ref, *refs, pps, n_steps, n_new):
    k_refs, v_refs = refs[:pps], refs[pps:2 * pps]
    kn_ref, vn_ref, o_ref, kbuf, vbuf, carry_sc, acc_sc = refs[2 * pps:]
    c = pl.program_id(1)
    rows = N_HEADS * n_new
    r_i = lax.broadcasted_iota(I32, (LANES, LANES), 0)
    c_i = lax.broadcasted_iota(I32, (LANES, LANES), 1)
    tri = jnp.where(r_i >= c_i, 1.0, 0.0).astype(BF16)

    def page_weights(z, lk, carry):
        hi = lk.astype(BF16)
        lo = (lk - hi.astype(F32)).astype(BF16)
        suffix = (jnp.dot(hi, tri, preferred_element_type=F32)
                  + jnp.dot(lo, tri, preferred_element_type=F32))
        return jnp.exp(z + suffix + carry), carry + jnp.sum(lk, axis=1, keepdims=True)

    @pl.when(c == 0)
    def _():
        lane = lax.broadcasted_iota(I32, (rows, LANES), 1)
        mask = lane < lax.broadcasted_iota(I32, (rows, LANES), 0) % n_new
        z = lax.dot_general(q_ref[0], kn_ref[0], (((1,), (1,)), ((), ())), preferred_element_type=F32)
        lk = jnp.where(mask, _neg_softplus(z), 0.0)
        a, carry = page_weights(z, lk, jnp.zeros((rows, 1), F32))
        a = jnp.where(mask, a, 0.0)
        acc_sc[...] = jnp.dot(a.astype(BF16), vn_ref[0], preferred_element_type=F32)
        carry_sc[...] = carry

    for p in range(pps):
        kbuf[p * PAGE_SIZE:(p + 1) * PAGE_SIZE, :] = k_refs[p][0].astype(BF16)
        vbuf[p * PAGE_SIZE:(p + 1) * PAGE_SIZE, :] = v_refs[p][0].astype(BF16)
    z = lax.dot_general(q_ref[0], kbuf[...], (((1,), (1,)), ((), ())), preferred_element_type=F32)
    lk = _neg_softplus(z)
    carry = carry_sc[...]
    parts = [None] * pps
    for p in reversed(range(pps)):
        cols = slice(p * PAGE_SIZE, (p + 1) * PAGE_SIZE)
        parts[p], carry = page_weights(z[:, cols], lk[:, cols], carry)
    a = jnp.concatenate(parts, axis=1)
    acc_sc[...] += jnp.dot(a.astype(BF16), vbuf[...], preferred_element_type=F32)
    carry_sc[...] = carry

    @pl.when(c == n_steps - 1)
    def _():
        o_ref[0] = _pick_heads(acc_sc[...], n_new)


def _sample_attention(page_table, qi_rows, wi_rows, qa_rows, qb_bd, new_pages, caches, top_k):
    db, n_pages = page_table.shape
    n_new = qi_rows.shape[1] // N_HEADS
    rows = N_HEADS * n_new
    pt_flat = page_table.reshape(-1)
    n_pos_bits = int(n_pages * PAGE_SIZE + PAGE_SIZE).bit_length()
    pps = min(PAGES_PER_STEP, n_pages)
    n_steps = n_pages // pps
    assert n_steps * pps == n_pages
    per_b = lambda b, *_: (b, 0, 0)

    def page_spec(width, page_of):
        return pl.BlockSpec((1, PAGE_SIZE, width), lambda b, *a: (a[-1][b * n_pages + page_of(*a[:-1])], 0, 0))

    idx_pages = [page_spec(IDX_DIM, lambda p=p: p) for p in range(n_pages)]
    bias = pl.pallas_call(
        functools.partial(_idx_sample_kernel, n_pages=n_pages, top_k=top_k, n_new=n_new,
                          n_pos_bits=n_pos_bits),
        grid_spec=pltpu.PrefetchScalarGridSpec(
            num_scalar_prefetch=1,
            grid=(db,),
            in_specs=[pl.BlockSpec((1, rows, IDX_DIM), per_b), pl.BlockSpec((1, rows, LANES), per_b)]
                     + idx_pages + [pl.BlockSpec((1, PAGE_SIZE, IDX_DIM), per_b)],
            out_specs=pl.BlockSpec((1, n_new, (n_pages + 1) * LANES), per_b),
            scratch_shapes=[pltpu.VMEM((n_pages + 1, n_new, LANES), I32)],
        ),
        out_shape=jax.ShapeDtypeStruct((db, n_new, (n_pages + 1) * LANES), F32),
        compiler_params=_cparams(("arbitrary",)),
        name="idx_sample",
    )(pt_flat, qi_rows, wi_rows, *([caches["k_idx"]] * n_pages), new_pages["k_idx"])

    fwd = [page_spec(WIDTH, lambda c, p=p: c * pps + p) for p in range(pps)]
    rev = [page_spec(WIDTH, lambda c, p=p: n_pages - pps * (c + 1) + p) for p in range(pps)]
    new_spec = pl.BlockSpec((1, PAGE_SIZE, WIDTH), per_b)
    q_spec = pl.BlockSpec((1, rows, WIDTH), per_b)
    o_spec = pl.BlockSpec((1, n_new, WIDTH), per_b)
    page_bufs = [pltpu.VMEM((pps * PAGE_SIZE, WIDTH), BF16)] * 2
    pps_a = min(PAGES_PER_STEP_STORED, n_pages)
    n_steps_a = n_pages // pps_a
    assert n_steps_a * pps_a == n_pages
    stored = [pl.BlockSpec((1, 1, PAGE_SIZE, N_HEADS, HEAD_DIM),
                           lambda b, c, pt, p=p: (0, pt[b * n_pages + c * pps_a + p], 0, 0, 0))
              for p in range(pps_a)]
    head_rows = pl.BlockSpec((1, rows, HEAD_DIM), per_b)
    oa = pl.pallas_call(
        functools.partial(_att_sample_kernel, pps=pps_a, n_steps=n_steps_a, n_new=n_new),
        grid_spec=pltpu.PrefetchScalarGridSpec(
            num_scalar_prefetch=1,
            grid=(db, n_steps_a),
            in_specs=[head_rows,
                      pl.BlockSpec((1, n_new, pps_a * LANES), lambda b, c, pt: (b, 0, c)),
                      pl.BlockSpec((1, n_new, LANES), lambda b, c, pt: (b, 0, n_pages))]
                     + stored + stored + [head_rows, head_rows],
            out_specs=head_rows,
            scratch_shapes=[pltpu.VMEM((pps_a * PAGE_SIZE * N_HEADS, HEAD_DIM), BF16)] * 2
                           + [pltpu.VMEM((rows, 1), F32), pltpu.VMEM((rows, 1), F32),
                              pltpu.VMEM((rows, HEAD_DIM), F32)],
        ),
        out_shape=jax.ShapeDtypeStruct((db, rows, HEAD_DIM), F32),
        compiler_params=_cparams(("arbitrary", "arbitrary")),
        name="att_sample",
    )(pt_flat, qa_rows, bias, bias, *([caches["k_a"]] * pps_a), *([caches["v_a"]] * pps_a),
      new_pages["k_a"], new_pages["v_a"])

    ob = pl.pallas_call(
        functools.partial(_sb_sample_kernel, pps=pps, n_steps=n_steps, n_new=n_new),
        grid_spec=pltpu.PrefetchScalarGridSpec(
            num_scalar_prefetch=1,
            grid=(db, n_steps),
            in_specs=[q_spec] + rev + rev + [new_spec, new_spec],
            out_specs=o_spec,
            scratch_shapes=page_bufs + [pltpu.VMEM((rows, 1), F32), pltpu.VMEM((rows, WIDTH), F32)],
        ),
        out_shape=jax.ShapeDtypeStruct((db, n_new, WIDTH), F32),
        compiler_params=_cparams(("arbitrary", "arbitrary")),
        name="sb_sample",
    )(pt_flat, qb_bd, *([caches["k_b"]] * pps), *([caches["v_b"]] * pps),
      new_pages["k_b"], new_pages["v_b"])
    return oa, ob


def _rms(x, g):
    ms = jnp.mean(x * x, axis=-1, keepdims=True)
    return (x * lax.rsqrt(ms + RMS_EPS)) * g


def _merge_kernel(x_ref, oa_ref, ob_ref, gate_ref, wa_ref, wb_ref, wo_ref, g_ref, x1_ref, h2_ref):
    pa = jnp.dot(oa_ref[0].astype(BF16), wa_ref[...], preferred_element_type=F32)
    pb = jnp.dot(ob_ref[0].astype(BF16), wb_ref[...], preferred_element_type=F32)
    ga = gate_ref[0, :, 0:D_MODEL]
    gb = gate_ref[0, :, D_MODEL:2 * D_MODEL]
    m = jax.nn.sigmoid(ga) * pa + jax.nn.sigmoid(gb) * pb
    x1 = x_ref[0] + jnp.dot(m.astype(BF16), wo_ref[...], preferred_element_type=F32)
    x1_ref[0] = x1
    h2_ref[0] = _rms(x1, g_ref[...]).astype(BF16)


def _merge(x3, oa, ob, gates, wa, wb, wo, g_ffn):
    groups, rows, _ = x3.shape
    blk = lambda n: pl.BlockSpec((1, TR, n), lambda b, t: (b, t, 0))
    const = lambda shape: _resident(shape, lambda b, t: (0,) * len(shape))
    return pl.pallas_call(
        _merge_kernel,
        grid=(groups, rows // TR),
        in_specs=[blk(D_MODEL), blk(WIDTH), blk(WIDTH), blk(2 * D_MODEL),
                  const((WIDTH, D_MODEL)), const((WIDTH, D_MODEL)), const((D_MODEL, D_MODEL)),
                  const((1, D_MODEL))],
        out_specs=[blk(D_MODEL), blk(D_MODEL)],
        out_shape=[jax.ShapeDtypeStruct((groups, rows, D_MODEL), F32),
                   jax.ShapeDtypeStruct((groups, rows, D_MODEL), BF16)],
        compiler_params=_cparams(("arbitrary", "arbitrary")),
        name="merge",
    )(x3, oa, ob, gates, wa, wb, wo, g_ffn)


def _ffn_kernel(*refs, seq_len, slab, has_state):
    if has_state:
        (x1_ref, h2_ref, wup_ref, cw_ref, cb_ref, wdn_ref, gf_ref, s1_ref, s2_ref,
         y_ref, a_ref, buf) = refs
    else:
        x1_ref, h2_ref, wup_ref, cw_ref, cb_ref, wdn_ref, gf_ref, y_ref, a_ref, buf = refs
    t = pl.program_id(1)
    u = jnp.dot(h2_ref[0], wup_ref[...], preferred_element_type=F32)
    a = u[:, :D_FF]
    b = u[:, D_FF:]

    @pl.when(t == 0)
    def _():
        buf[0:SUBLANES, :] = jnp.zeros((SUBLANES, D_FF), F32)

    buf[SUBLANES:SUBLANES + TR, :] = a
    prev1 = buf[SUBLANES - 1:SUBLANES - 1 + TR, :]
    prev2 = buf[SUBLANES - 2:SUBLANES - 2 + TR, :]
    if has_state:
        tok = lax.broadcasted_iota(I32, (TR, 1), 0) % seq_len
        prev1 = jnp.where(tok == 0, s1_ref[...], prev1)
        prev2 = jnp.where(tok <= 1, s2_ref[...], prev2)
    cw = cw_ref[...]
    a_c = cb_ref[...] + cw[0:1, :] * prev2 + cw[1:2, :] * prev1 + cw[2:3, :] * a
    gelu = 0.5 * a_c * (1.0 + lax.erf(a_c * (2.0 ** -0.5)))
    f = jnp.dot((gelu * b).astype(BF16), wdn_ref[...], preferred_element_type=F32)
    y_ref[0] = _rms(x1_ref[0] + f, gf_ref[...])
    a_ref[0] = a[slab[0]:slab[0] + slab[1], :]
    buf[0:SUBLANES, :] = a[TR - SUBLANES:, :]


def _ffn(x1, h2, w_up, conv_w, conv_b, w_down, g_final, slab, state=None, seq_len=0):
    nseq, rows = x1.shape[0], x1.shape[1]
    nt = rows // TR
    blk = lambda n: pl.BlockSpec((1, TR, n), lambda s, t: (s, t, 0))
    const = lambda shape: _resident(shape, lambda s, t: (0,) * len(shape))
    in_specs = [blk(D_MODEL), blk(D_MODEL), const((D_MODEL, 2 * D_FF)), const((3, D_FF)),
                const((1, D_FF)), const((D_FF, D_MODEL)), const((1, D_MODEL))]
    args = [x1, h2, w_up, conv_w, conv_b, w_down, g_final]
    if state is not None:
        in_specs += [const((TR, D_FF)), const((TR, D_FF))]
        args += list(state)
    return pl.pallas_call(
        functools.partial(_ffn_kernel, seq_len=seq_len, slab=slab, has_state=state is not None),
        grid=(nseq, nt),
        in_specs=in_specs,
        out_specs=[blk(D_MODEL), pl.BlockSpec((1, slab[1], D_FF), lambda s, t: (s, t, 0))],
        out_shape=[jax.ShapeDtypeStruct((nseq, rows, D_MODEL), F32),
                   jax.ShapeDtypeStruct((nseq, nt * slab[1], D_FF), F32)],
        scratch_shapes=[pltpu.VMEM((SUBLANES + TR, D_FF), F32)],
        compiler_params=_cparams(("arbitrary", "arbitrary")),
        name="conv_ffn",
    )(*args)


def _rope_tables(positions):
    half = HEAD_DIM // 2
    inv = ROPE_THETA ** (-np.arange(half, dtype=np.float64) * 2.0 / HEAD_DIM)
    ang = np.asarray(positions, np.float64)[:, None] * inv[None, :]
    cos = np.tile(np.cos(ang), (1, LANES // half))
    sin = np.sin(ang)
    sin = np.tile(np.concatenate([-sin, sin], axis=1), (1, LANES // HEAD_DIM))
    return cos.astype(np.float32), sin.astype(np.float32)


def _permute_w_in(w):
    off = np.cumsum([0, WIDTH, WIDTH, WIDTH, WIDTH, IDX_DIM, N_HEADS, WIDTH, WIDTH, WIDTH,
                     D_MODEL, D_MODEL])
    qa, ka, va, qi, ki, wi, qb, kb, vb, ga, gb = [w[:, off[j]:off[j + 1]] for j in range(11)]
    pad = jnp.zeros((w.shape[0], C_VA - C_WI - N_HEADS), w.dtype)
    out = jnp.concatenate([ga, gb, qa, ka, qi, ki, wi, pad, va, qb, kb, vb], axis=1)
    assert out.shape[1] == NW
    return out.astype(BF16)


def _block_diag_rows(q):
    db, n_new, _ = q.shape
    head_of_col = jnp.arange(WIDTH) // HEAD_DIM
    keep = head_of_col[None, :] == jnp.arange(N_HEADS)[:, None]
    out = jnp.where(keep[None, :, None, :], q[:, None, :, :], 0)
    return out.reshape(db, N_HEADS * n_new, WIDTH)


def _pad_page(x):
    return jnp.pad(x, ((0, 0), (0, PAGE_SIZE - x.shape[1]), (0, 0)))


def kernel(x_prompt, x_sample, cache_k_a, cache_v_a, cache_k_idx, cache_k_b, cache_v_b, state_conv,
           page_table, meta_tokens, g_attn, w_in, w_branch_a, w_branch_b, w_o, g_ffn, w_up, conv_w,
           conv_b, w_down, g_final):
    batch, seq = x_prompt.shape[0], x_prompt.shape[1]
    db, n_new = x_sample.shape[0], x_sample.shape[1]
    n_pages = page_table.shape[1]
    past_len = n_pages * PAGE_SIZE
    s_len = N_META + seq
    s_pad = -(-s_len // TQ) * TQ
    top_k_p = min(TOP_K_MAX, seq // 4)
    top_k_s = min(TOP_K_MAX, (past_len + n_new) // 4)
    assert db * n_new == TR and cache_k_a.shape[0] == 1
    assert (s_len - 2) // SUBLANES == (s_len - 1) // SUBLANES

    meta = jnp.broadcast_to(meta_tokens[None], (batch, N_META, D_MODEL))
    xp = jnp.concatenate([meta, x_prompt, jnp.zeros((batch, s_pad - s_len, D_MODEL), F32)], axis=1)
    xs = x_sample.reshape(1, TR, D_MODEL)
    w_perm = _permute_w_in(w_in[0])
    cos_p, sin_p = _rope_tables(np.arange(s_pad))
    cos_s, sin_s = _rope_tables(past_len + np.arange(TR) % n_new)

    (ka_p, va_p, ki_p, kb_p, vb_p, qa_pb, qi_pb, qb_pb, ka_pb, va_pb, kb_pb, vb_pb, ki_pb, kiwi_p,
     gate_p) = _in_projection(xp, g_attn, w_perm, jnp.asarray(cos_p), jnp.asarray(sin_p),
                              lambda b, t: t, s_len)
    (ka_s, va_s, ki_s, kb_s, vb_s, qa_sb, qi_sb, qb_sb, ka_sb, va_sb, kb_sb, vb_sb, ki_sb, kiwi_s,
     gate_s) = _in_projection(xs, g_attn, w_perm, jnp.asarray(cos_s), jnp.asarray(sin_s),
                              lambda b, t: t, TR)

    oa_p = _dsa_prompt(qa_pb, qi_pb, kiwi_p, ka_pb, ki_pb, va_pb, top_k_p)
    ob_p = _sb_prompt(qb_pb, kb_pb, vb_pb)

    per_seq = lambda a: a.reshape(db, n_new, a.shape[-1])
    qi_rows = per_seq(qi_sb).reshape(db, n_new, N_HEADS, IDX_DIM).transpose(0, 2, 1, 3)
    qi_rows = qi_rows.reshape(db, N_HEADS * n_new, IDX_DIM)
    wi_s = per_seq(kiwi_s)[..., C_WI - C_KI:C_WI - C_KI + N_HEADS]
    wi_rows = jnp.broadcast_to(wi_s.transpose(0, 2, 1).reshape(db, N_HEADS * n_new, 1),
                               (db, N_HEADS * n_new, LANES))
    n_phys = cache_k_a.shape[1]
    head_rows = lambda a: per_seq(a).reshape(db, n_new, N_HEADS, HEAD_DIM)
    caches = {
        "k_a": cache_k_a, "v_a": cache_v_a,
        "k_idx": cache_k_idx.reshape(n_phys, PAGE_SIZE, IDX_DIM),
        "k_b": cache_k_b.reshape(n_phys, PAGE_SIZE, WIDTH).astype(BF16),
        "v_b": cache_v_b.reshape(n_phys, PAGE_SIZE, WIDTH).astype(BF16),
    }
    new_pages = {name: _pad_page(per_seq(arr)) for name, arr in
                 (("k_idx", ki_sb), ("k_b", kb_sb), ("v_b", vb_sb))}
    new_pages["k_a"] = head_rows(ka_sb).reshape(db, n_new * N_HEADS, HEAD_DIM)
    new_pages["v_a"] = head_rows(va_sb).reshape(db, n_new * N_HEADS, HEAD_DIM)
    qa_rows = head_rows(qa_sb).transpose(0, 2, 1, 3).reshape(db, N_HEADS * n_new, HEAD_DIM)
    oa_s, ob_s = _sample_attention(page_table, qi_rows, wi_rows, qa_rows,
                                   _block_diag_rows(per_seq(qb_sb)), new_pages, caches, top_k_s)
    oa_s = oa_s.reshape(db, N_HEADS, n_new, HEAD_DIM).transpose(0, 2, 1, 3)

    merge_w = (w_branch_a[0].astype(BF16), w_branch_b[0].astype(BF16), w_o[0].astype(BF16), g_ffn)
    x1_p, h2_p = _merge(xp, oa_p, ob_p, gate_p, *merge_w)
    x1_s, h2_s = _merge(xs, oa_s.reshape(1, TR, WIDTH), ob_s.reshape(1, TR, WIDTH), gate_s, *merge_w)
    ffn_w = (w_up[0].astype(BF16), conv_w[0], conv_b, w_down[0].astype(BF16), g_final[None])
    tail_tile, tail_off = (s_len - 2) // TR, (s_len - 2) % TR
    slab_off = tail_off // SUBLANES * SUBLANES
    y_p, a_p = _ffn(x1_p, h2_p, *ffn_w, slab=(slab_off, SUBLANES))
    st = state_conv[0]
    zeros = jnp.zeros((db, n_new - 1, D_FF), F32)
    s1 = jnp.concatenate([st[:, 1:2], zeros], axis=1).reshape(TR, D_FF)
    s2 = jnp.concatenate([st[:, 0:1], st[:, 1:2], zeros[:, 1:]], axis=1).reshape(TR, D_FF)
    y_s, a_s = _ffn(x1_s, h2_s, *ffn_w, slab=(0, TR), state=(s1, s2), seq_len=n_new)

    y_prompt = y_p[:, N_META:s_len]
    y_sample = y_s.reshape(db, n_new, D_MODEL)
    a_tail = tail_tile * SUBLANES + tail_off - slab_off
    conv_p = a_p[:, a_tail:a_tail + 2][None]
    conv_s = a_s.reshape(db, n_new, D_FF)[:, n_new - 2:][None]
    heads_p = lambda a: a.reshape(1, batch, s_len, N_HEADS, HEAD_DIM)
    heads_s = lambda a: a.reshape(1, db, n_new, N_HEADS, HEAD_DIM)
    return (y_prompt, y_sample,
            heads_p(ka_p), heads_p(va_p), ki_p[None], heads_p(kb_p), heads_p(vb_p), conv_p,
            heads_s(ka_s), heads_s(va_s), ki_s.reshape(1, db, n_new, IDX_DIM),
            heads_s(kb_s), heads_s(vb_s), conv_s)
```

```python
import functools

import numpy as np
import jax
import jax.numpy as jnp
from jax import lax
from jax.experimental import pallas as pl
from jax.experimental.pallas import tpu as pltpu

F32 = jnp.float32
BF16 = jnp.bfloat16
I32 = jnp.int32

D_MODEL = 1024
N_HEADS = 8
HEAD_DIM = 64
WIDTH = N_HEADS * HEAD_DIM
IDX_DIM = 64
N_META = 16
PAGE_SIZE = 128
TOP_K_MAX = 256
ROPE_THETA = 10000.0
D_FF = 2816
RMS_EPS = 1e-6

LANES = 128
SUBLANES = 8
TQ = 256
TK = 256
TR = 256
RUN = TK // SUBLANES
PAGES_PER_STEP = 16
VMEM_LIMIT = 56 * 1024 * 1024

INT_MIN = -2 ** 31
NEG_BIG = -1e30
EXP_ZERO = -104.0

C_GA, C_GB = 0, 1024
C_QA, C_KA, C_QI = 2048, 2560, 3072
C_KI, C_WI = 3584, 3648
C_VA, C_QB, C_KB, C_VB = 3712, 4224, 4736, 5248
NW = 5760
ROPE_LO, ROPE_HI = C_QA, C_VA


def _cparams(sem):
    return pltpu.CompilerParams(dimension_semantics=sem, vmem_limit_bytes=VMEM_LIMIT)


def _resident(shape, index_map):
    return pl.BlockSpec(shape, index_map, pipeline_mode=pl.Buffered(1))


def _inproj_kernel(x_ref, g_ref, w_ref, cos_ref, sin_ref,
                   ka_f, va_f, ki_f, kb_f, vb_f,
                   qa_b, qi_b, qb_b, ka_b, va_b, kb_b, vb_b, ki_b, kiwi_f, gate_f):
    x = x_ref[0]
    ms = jnp.mean(x * x, axis=-1, keepdims=True)
    h = (x * lax.rsqrt(ms + RMS_EPS)) * g_ref[...]
    y = jnp.dot(h.astype(BF16), w_ref[...], preferred_element_type=F32)
    cos = cos_ref[...]
    sin = sin_ref[...]
    lane = lax.broadcasted_iota(I32, (1, LANES), 1)
    first_half = (lane % HEAD_DIM) < (HEAD_DIM // 2)
    q_scale = HEAD_DIM ** -0.5
    w_lanes = (lane >= C_WI - C_KI) & (lane < C_WI - C_KI + N_HEADS)

    def rope(yc):
        partner = jnp.where(first_half, pltpu.roll(yc, LANES - HEAD_DIM // 2, 1),
                            pltpu.roll(yc, HEAD_DIM // 2, 1))
        return yc * cos + partner * sin

    plan = [(C_GA, 2 * D_MODEL, False, 1.0, gate_f, None),
            (C_QA, WIDTH, True, q_scale, None, qa_b), (C_KA, WIDTH, True, 1.0, ka_f, ka_b),
            (C_QI, WIDTH, True, q_scale, None, qi_b),
            (C_VA, WIDTH, False, 1.0, va_f, va_b), (C_QB, WIDTH, False, q_scale, None, qb_b),
            (C_KB, WIDTH, False, 1.0, kb_f, kb_b), (C_VB, WIDTH, False, 1.0, vb_f, vb_b)]
    for c_lo, width, rotary, scale, dst_f, dst_b in plan:
        for off in range(0, width, LANES):
            r = y[:, c_lo + off:c_lo + off + LANES]
            if rotary:
                r = rope(r)
            if scale != 1.0:
                r = r * scale
            if dst_f is not None:
                dst_f[0, :, off:off + LANES] = r
            if dst_b is not None:
                dst_b[0, :, off:off + LANES] = r.astype(BF16)
    yc = y[:, C_KI:C_KI + LANES]
    r = jnp.where(lane < IDX_DIM, rope(yc), yc * jnp.where(w_lanes, N_HEADS ** -0.5, 1.0).astype(F32))
    kiwi_f[0] = r
    ki_f[0] = r[:, :IDX_DIM]
    ki_b[0] = r[:, :IDX_DIM].astype(BF16)


def _in_projection(x3, g, w_perm, cos_tab, sin_tab, tab_index, leaf_len):
    groups, rows, _ = x3.shape
    blk = lambda n: pl.BlockSpec((1, TR, n), lambda b, t: (b, t, 0))
    leaf = lambda n: jax.ShapeDtypeStruct((groups, leaf_len, n), F32)
    full = lambda n, dt: jax.ShapeDtypeStruct((groups, rows, n), dt)
    tab_spec = pl.BlockSpec((TR, LANES), lambda b, t: (tab_index(b, t), 0))
    return pl.pallas_call(
        _inproj_kernel,
        grid=(groups, rows // TR),
        in_specs=[blk(D_MODEL), _resident((1, D_MODEL), lambda b, t: (0, 0)),
                  _resident((D_MODEL, NW), lambda b, t: (0, 0)), tab_spec, tab_spec],
        out_specs=[blk(WIDTH), blk(WIDTH), blk(IDX_DIM), blk(WIDTH), blk(WIDTH)]
                  + [blk(WIDTH)] * 7 + [blk(IDX_DIM), blk(LANES), blk(2 * D_MODEL)],
        out_shape=[leaf(WIDTH), leaf(WIDTH), leaf(IDX_DIM), leaf(WIDTH), leaf(WIDTH)]
                  + [full(WIDTH, BF16)] * 7 + [full(IDX_DIM, BF16), full(LANES, F32),
                                               full(2 * D_MODEL, F32)],
        compiler_params=_cparams(("arbitrary", "arbitrary")),
        name="in_projection",
    )(x3, g, w_perm, cos_tab, sin_tab)


def _sortable_key(score):
    bits = lax.bitcast_convert_type(score, I32)
    return bits ^ ((bits >> 31) & jnp.int32(0x7FFFFFFF))


def _topk_select(count_fn, shape, top_k, n_pos_bits):
    def bit_step(it, thr):
        cand = thr + (jnp.int32(1) << (31 - it))
        cnt = count_fn(lambda key, kpos: key >= cand)
        return jnp.where(cnt >= top_k, cand, thr)

    thr = lax.fori_loop(0, 32, bit_step, jnp.full(shape, INT_MIN, I32))
    c_ge = count_fn(lambda key, kpos: key >= thr)
    c_gt = count_fn(lambda key, kpos: key > thr)
    need = top_k - c_gt
    has_tie = (c_ge > top_k) & (thr > INT_MIN)

    def pos_step(it, x):
        cand = x + (jnp.int32(1) << (n_pos_bits - 1 - it))
        cnt = count_fn(lambda key, kpos: (key == thr) & (kpos < cand))
        return jnp.where(cnt < need, cand, x)

    def tie_search():
        return lax.fori_loop(0, n_pos_bits, pos_step, jnp.zeros(shape, I32))

    any_tie = jnp.max(has_tie.astype(I32)) > 0
    tie_x = lax.cond(any_tie, tie_search, lambda: jnp.zeros(shape, I32))
    big = jnp.int32(2 ** 30)
    tie_pos = jnp.where(thr == INT_MIN, -1, jnp.where(has_tie, tie_x, big))
    return thr, tie_pos


def _masked_pairs(q_t, qm_sc):
    zeros = jnp.zeros((HEAD_DIM, q_t.shape[1]), F32)
    for h in range(N_HEADS):
        part = q_t[h * HEAD_DIM:(h + 1) * HEAD_DIM]
        pair = [part, zeros] if h % 2 == 0 else [zeros, part]
        qm_sc[h] = jnp.concatenate(pair, axis=0).astype(BF16)


def _dsa_prompt_kernel(qa_ref, qi_ref, kiwi_ref, ka_ref, ki_ref, va_ref, o_ref,
                       keys_sc, vt_sc, qm_sc, qit_sc, m_sc, l_sc, acc_sc, s_sc,
                       *, top_k, n_pos_bits, n_key_blocks):
    i = pl.program_id(1)
    nkb = i + 1
    row = lax.broadcasted_iota(I32, (TK, TQ), 0)
    col = lax.broadcasted_iota(I32, (TK, TQ), 1)

    @pl.when(i == 0)
    def _():
        def transpose_block(kb, carry):
            blk = va_ref[0, pl.ds(pl.multiple_of(kb * TK, TK), TK), :]
            vt_sc[kb] = blk.astype(F32).T.astype(BF16)
            return carry
        lax.fori_loop(0, n_key_blocks, transpose_block, 0)

    _masked_pairs(qa_ref[0].astype(F32).T, qm_sc)
    qit_sc[...] = qi_ref[0].astype(F32).T.astype(BF16)
    wit = kiwi_ref[0].T[C_WI - C_KI:C_WI - C_KI + N_HEADS, :]

    def score_block(kb, carry):
        k_idx = ki_ref[0, pl.ds(pl.multiple_of(kb * TK, TK), TK), :]
        sc = jnp.zeros((TK, TQ), F32)
        for h in range(N_HEADS):
            r = jnp.dot(k_idx, qit_sc[h * IDX_DIM:(h + 1) * IDX_DIM, :], preferred_element_type=F32)
            sc = sc + wit[h:h + 1, :] * jnp.maximum(r, 0.0)
        keys_sc[kb] = _sortable_key(sc)
        return carry

    lax.fori_loop(0, nkb, score_block, 0)
    keys_sc[i] = jnp.where(row <= col, keys_sc[i], INT_MIN)

    def count_fn(pred):
        def body(kb, acc):
            m = pred(keys_sc[kb], kb * TK + row).astype(I32)
            for g in range(TK // SUBLANES):
                acc = acc + m[g * SUBLANES:(g + 1) * SUBLANES]
            return acc
        acc = lax.fori_loop(0, nkb, body, jnp.zeros((SUBLANES, TQ), I32))
        return jnp.sum(acc, axis=0, keepdims=True)

    thr, tie_pos = _topk_select(count_fn, (1, TQ), top_k, n_pos_bits)

    m_sc[...] = jnp.full(m_sc.shape, NEG_BIG, F32)
    l_sc[...] = jnp.zeros(l_sc.shape, F32)
    acc_sc[...] = jnp.zeros(acc_sc.shape, F32)

    def attend_block(kb, carry):
        key = keys_sc[kb]
        sel = (key > thr) | ((key == thr) & ((kb * TK + row) <= tie_pos))
        bias = jnp.where(sel, 0.0, NEG_BIG).astype(F32)
        start = pl.multiple_of(kb * TK, TK)

        m_new = []
        for h in range(N_HEADS):
            k_pair = ka_ref[0, pl.ds(start, TK), (h // 2) * LANES:(h // 2 + 1) * LANES]
            s = jnp.dot(k_pair, qm_sc[h], preferred_element_type=F32) + bias
            s_sc[h] = s
            m_new.append(jnp.maximum(m_sc[h:h + 1, :], jnp.max(s, axis=0, keepdims=True)))
        for h in range(N_HEADS):
            p = jnp.exp(s_sc[h] - m_new[h])
            alpha = jnp.exp(m_sc[h:h + 1, :] - m_new[h])
            l_sc[h:h + 1, :] = alpha * l_sc[h:h + 1, :] + jnp.sum(p, axis=0, keepdims=True)
            pv = jnp.dot(vt_sc[kb, h * HEAD_DIM:(h + 1) * HEAD_DIM, :], p.astype(BF16),
                         preferred_element_type=F32)
            rows = slice(h * HEAD_DIM, (h + 1) * HEAD_DIM)
            acc_sc[rows, :] = alpha * acc_sc[rows, :] + pv
            m_sc[h:h + 1, :] = m_new[h]
        return carry

    lax.fori_loop(0, nkb, attend_block, 0)
    for h in range(N_HEADS):
        rows = slice(h * HEAD_DIM, (h + 1) * HEAD_DIM)
        acc_sc[rows, :] = acc_sc[rows, :] / l_sc[h:h + 1, :]
    o_ref[0] = acc_sc[...].T


def _dsa_prompt(qa, qi, kiwi, ka, ki, va, top_k):
    b, s_pad, _ = ka.shape
    nkb = s_pad // TK
    kern = functools.partial(_dsa_prompt_kernel, top_k=top_k, n_pos_bits=int(s_pad).bit_length(),
                             n_key_blocks=nkb)
    q_blk = lambda n: pl.BlockSpec((1, TQ, n), lambda bb, i: (bb, i, 0))
    return pl.pallas_call(
        kern,
        grid=(b, s_pad // TQ),
        in_specs=[q_blk(WIDTH), q_blk(WIDTH), q_blk(LANES),
                  _resident((1, s_pad, WIDTH), lambda bb, i: (bb, 0, 0)),
                  _resident((1, s_pad, IDX_DIM), lambda bb, i: (bb, 0, 0)),
                  _resident((1, s_pad, WIDTH), lambda bb, i: (bb, 0, 0))],
        out_specs=q_blk(WIDTH),
        out_shape=jax.ShapeDtypeStruct((b, s_pad, WIDTH), F32),
        scratch_shapes=[
            pltpu.VMEM((nkb, TK, TQ), I32),
            pltpu.VMEM((nkb, WIDTH, TK), BF16),
            pltpu.VMEM((N_HEADS, LANES, TQ), BF16),
            pltpu.VMEM((WIDTH, TQ), BF16),
            pltpu.VMEM((N_HEADS, TQ), F32),
            pltpu.VMEM((N_HEADS, TQ), F32),
            pltpu.VMEM((WIDTH, TQ), F32),
            pltpu.VMEM((N_HEADS, TK, TQ), F32),
        ],
        compiler_params=_cparams(("arbitrary", "arbitrary")),
        name="dsa_prompt",
    )(qa, qi, kiwi, ka, ki, va)


def _neg_softplus(z):
    return -(jnp.maximum(z, 0.0) + jnp.log(1.0 + jnp.exp(-jnp.abs(z))))


def _sb_prompt_kernel(qb_ref, kb_ref, vb_ref, o_ref,
                      kperm_sc, vt_sc, stage_sc, qm_sc, kinf_sm, acc_sc, *, n_key_blocks):
    i = pl.program_id(1)
    row = lax.broadcasted_iota(I32, (TK, TQ), 0)
    col = lax.broadcasted_iota(I32, (TK, TQ), 1)
    key_off = RUN * (row % SUBLANES) + row // SUBLANES
    strict = key_off < col
    sub = lax.broadcasted_iota(I32, (SUBLANES, TQ), 0)

    def permuted(block_f32):
        cols = []
        for c in range(WIDTH // LANES):
            stage_sc[c] = block_f32[:, c * LANES:(c + 1) * LANES]
            cols.append(jnp.concatenate(
                [stage_sc[c, pl.ds(v, SUBLANES, stride=RUN), :] for v in range(RUN)], axis=0))
        return jnp.concatenate(cols, axis=1)

    @pl.when(i == 0)
    def _():
        def prep(kb, kmax):
            start = pl.multiple_of(kb * TK, TK)
            k_blk = kb_ref[0, pl.ds(start, TK), :].astype(F32)
            k_abs = jnp.abs(k_blk)
            for g in range(TK // SUBLANES):
                kmax = jnp.maximum(kmax, k_abs[g * SUBLANES:(g + 1) * SUBLANES])
            kperm_sc[pl.ds(start, TK), :] = permuted(k_blk).astype(BF16)
            vt_sc[kb] = permuted(vb_ref[0, pl.ds(start, TK), :].astype(F32)).T.astype(BF16)
            return kmax
        kmax = lax.fori_loop(0, n_key_blocks, prep, jnp.zeros((SUBLANES, WIDTH), F32))
        col_max = jnp.max(kmax, axis=0, keepdims=True)
        for h in range(N_HEADS):
            kinf_sm[h] = jnp.max(col_max[:, h * HEAD_DIM:(h + 1) * HEAD_DIM])

    q_t = qb_ref[0].astype(F32).T
    _masked_pairs(q_t, qm_sc)

    def block(z, lk, carry):
        run_sums = []
        acc = jnp.zeros((SUBLANES, TQ), F32)
        for v in reversed(range(RUN)):
            acc = acc + lk[v * SUBLANES:(v + 1) * SUBLANES]
            run_sums.append(acc)
        run_sums = run_sums[::-1]
        tot = run_sums[0]
        later = jnp.zeros((SUBLANES, TQ), F32)
        for k in range(1, SUBLANES):
            shifted = pltpu.roll(tot, SUBLANES - k, 0)
            later = later + jnp.where(sub + k < SUBLANES, shifted, 0.0)
        offs = later + carry
        parts = [jnp.exp(z[v * SUBLANES:(v + 1) * SUBLANES] + run_sums[v] + offs) for v in range(RUN)]
        return jnp.concatenate(parts, axis=0), carry + jnp.sum(tot, axis=0, keepdims=True)

    for pair in range(N_HEADS // 2):
        heads = (2 * pair, 2 * pair + 1)
        lanes = slice(pair * LANES, (pair + 1) * LANES)
        bounds = [jnp.sum(jnp.abs(q_t[h * HEAD_DIM:(h + 1) * HEAD_DIM]), axis=0, keepdims=True)
                  * (kinf_sm[h] * (1.0 + 2.0 ** -8)) for h in heads]

        def live(carries):
            worst = jnp.maximum(carries[0] + bounds[0], carries[1] + bounds[1])
            return jnp.max(worst) > EXP_ZERO

        start = pl.multiple_of(i * TK, TK)
        k_pair = kperm_sc[pl.ds(start, TK), lanes]
        carries = []
        for h in heads:
            rows = slice(h * HEAD_DIM, (h + 1) * HEAD_DIM)
            z = jnp.dot(k_pair, qm_sc[h], preferred_element_type=F32)
            lk = jnp.where(strict, _neg_softplus(z), 0.0)
            a, carry = block(z, lk, jnp.zeros((1, TQ), F32))
            a = jnp.where(strict, a, 0.0)
            acc_sc[rows, :] = jnp.dot(vt_sc[i, rows, :], a.astype(BF16), preferred_element_type=F32)
            carries.append(carry)

        def cond(state):
            kb, _, _, go = state
            return (kb >= 0) & go

        def earlier(state):
            kb, c0, c1, _ = state
            st = pl.multiple_of(kb * TK, TK)
            kp = kperm_sc[pl.ds(st, TK), lanes]
            new = []
            for h, carry in zip(heads, (c0, c1)):
                rows = slice(h * HEAD_DIM, (h + 1) * HEAD_DIM)
                zz = jnp.dot(kp, qm_sc[h], preferred_element_type=F32)
                aa, carry = block(zz, _neg_softplus(zz), carry)
                acc_sc[rows, :] += jnp.dot(vt_sc[kb, rows, :], aa.astype(BF16),
                                           preferred_element_type=F32)
                new.append(carry)
            return kb - 1, new[0], new[1], live(new)

        lax.while_loop(cond, earlier, (i - 1, carries[0], carries[1], live(carries)))
    o_ref[0] = acc_sc[...].T


def _sb_prompt(qb, kb, vb):
    b, s_pad, _ = kb.shape
    nkb = s_pad // TK
    return pl.pallas_call(
        functools.partial(_sb_prompt_kernel, n_key_blocks=nkb),
        grid=(b, s_pad // TQ),
        in_specs=[pl.BlockSpec((1, TQ, WIDTH), lambda bb, i: (bb, i, 0)),
                  _resident((1, s_pad, WIDTH), lambda bb, i: (bb, 0, 0)),
                  _resident((1, s_pad, WIDTH), lambda bb, i: (bb, 0, 0))],
        out_specs=pl.BlockSpec((1, TQ, WIDTH), lambda bb, i: (bb, i, 0)),
        out_shape=jax.ShapeDtypeStruct((b, s_pad, WIDTH), F32),
        scratch_shapes=[
            pltpu.VMEM((s_pad, WIDTH), BF16),
            pltpu.VMEM((nkb, WIDTH, TK), BF16),
            pltpu.VMEM((WIDTH // LANES, TK, LANES), F32),
            pltpu.VMEM((N_HEADS, LANES, TQ), BF16),
            pltpu.SMEM((N_HEADS,), F32),
            pltpu.VMEM((WIDTH, TQ), F32),
        ],
        compiler_params=_cparams(("arbitrary", "arbitrary")),
        name="sb_prompt",
    )(qb, kb, vb)


def _tile_rows(x, n):
    return jnp.concatenate([x] * n, axis=0)


def _pick_heads(o, n_new):
    head_of_col = lax.broadcasted_iota(I32, (n_new, WIDTH), 1) // HEAD_DIM
    out = jnp.zeros((n_new, WIDTH), F32)
    for h in range(N_HEADS):
        out = jnp.where(head_of_col == h, o[h * n_new:(h + 1) * n_new], out)
    return out


def _idx_sample_kernel(pt_ref, qi_ref, wi_ref, *refs, n_pages, top_k, n_new, n_pos_bits):
    page_refs = refs[:n_pages]
    kn_ref, bias_ref, keys_sc = refs[n_pages:]
    lane = lax.broadcasted_iota(I32, (n_new, LANES), 1)
    qrow = lax.broadcasted_iota(I32, (n_new, LANES), 0)

    def page_scores(k_page_t):
        r = jnp.dot(qi_ref[0], k_page_t, preferred_element_type=F32)
        rel = jnp.maximum(r, 0.0) * wi_ref[0]
        sc = jnp.zeros((n_new, LANES), F32)
        for h in range(N_HEADS):
            sc = sc + rel[h * n_new:(h + 1) * n_new]
        return _sortable_key(sc)

    for p in range(n_pages):
        keys_sc[p] = page_scores(page_refs[p][0].astype(BF16))
    keys_sc[n_pages] = jnp.where(lane <= qrow, page_scores(kn_ref[0]), INT_MIN)

    def count_fn(pred):
        def body(pp, acc):
            return acc + pred(keys_sc[pp], pp * PAGE_SIZE + lane).astype(I32)
        acc = lax.fori_loop(0, n_pages + 1, body, jnp.zeros((n_new, LANES), I32))
        return jnp.sum(acc, axis=1, keepdims=True)

    thr, tie_pos = _topk_select(count_fn, (n_new, 1), top_k, n_pos_bits)
    for p in range(n_pages + 1):
        key = keys_sc[p]
        sel = (key > thr) | ((key == thr) & ((p * PAGE_SIZE + lane) <= tie_pos))
        bias_ref[0, :, p * LANES:(p + 1) * LANES] = jnp.where(sel, 0.0, NEG_BIG).astype(F32)


def _att_sample_kernel(pt_ref, q_ref, bias_ref, bias_new_ref, *refs, pps, n_steps, n_new):
    k_refs, v_refs = refs[:pps], refs[pps:2 * pps]
    kn_ref, vn_ref, o_ref, kbuf, vbuf, m_sc, l_sc, acc_sc = refs[2 * pps:]
    c = pl.program_id(1)

    @pl.when(c == 0)
    def _():
        m_sc[...] = jnp.full(m_sc.shape, NEG_BIG, F32)
        l_sc[...] = jnp.zeros(l_sc.shape, F32)
        acc_sc[...] = jnp.zeros(acc_sc.shape, F32)

    def process(k_t, v_t, bias):
        s = jnp.dot(q_ref[0], k_t, preferred_element_type=F32) + _tile_rows(bias, N_HEADS)
        m_old = m_sc[...]
        m_new = jnp.maximum(m_old, jnp.max(s, axis=1, keepdims=True))
        pr = jnp.exp(s - m_new)
        alpha = jnp.exp(m_old - m_new)
        l_sc[...] = alpha * l_sc[...] + jnp.sum(pr, axis=1, keepdims=True)
        pv = lax.dot_general(pr.astype(BF16), v_t, (((1,), (1,)), ((), ())), preferred_element_type=F32)
        acc_sc[...] = alpha * acc_sc[...] + pv
        m_sc[...] = m_new

    for p in range(pps):
        kbuf[:, p * PAGE_SIZE:(p + 1) * PAGE_SIZE] = k_refs[p][0].astype(BF16)
        vbuf[:, p * PAGE_SIZE:(p + 1) * PAGE_SIZE] = v_refs[p][0].astype(BF16)
    process(kbuf[...], vbuf[...], bias_ref[0])

    @pl.when(c == n_steps - 1)
    def _():
        process(kn_ref[0], vn_ref[0], bias_new_ref[0])
        o_ref[0] = _pick_heads(acc_sc[...] / l_sc[...], n_new)


def _sb_sample_kernel(pt_ref, q_ref, *refs, pps, n_steps, n_new):
    k_refs, v_refs = refs[:pps], refs[pps:2 * pps]
    kn_ref, vn_ref, o_ref, kbuf, vbuf, carry_sc, acc_sc = refs[2 * pps:]
    c = pl.program_id(1)
    rows = N_HEADS * n_new
    r_i = lax.broadcasted_iota(I32, (LANES, LANES), 0)
    c_i = lax.broadcasted_iota(I32, (LANES, LANES), 1)
    tri = jnp.where(r_i >= c_i, 1.0, 0.0).astype(BF16)

    def page_weights(z, lk, carry):
        hi = lk.astype(BF16)
        lo = (lk - hi.astype(F32)).astype(BF16)
        suffix = (jnp.dot(hi, tri, preferred_element_type=F32)
                  + jnp.dot(lo, tri, preferred_element_type=F32))
        return jnp.exp(z + suffix + carry), carry + jnp.sum(lk, axis=1, keepdims=True)

    @pl.when(c == 0)
    def _():
        lane = lax.broadcasted_iota(I32, (rows, LANES), 1)
        mask = lane < lax.broadcasted_iota(I32, (rows, LANES), 0) % n_new
        z = jnp.dot(q_ref[0], kn_ref[0], preferred_element_type=F32)
        lk = jnp.where(mask, _neg_softplus(z), 0.0)
        a, carry = page_weights(z, lk, jnp.zeros((rows, 1), F32))
        a = jnp.where(mask, a, 0.0)
        acc_sc[...] = lax.dot_general(a.astype(BF16), vn_ref[0], (((1,), (1,)), ((), ())),
                                      preferred_element_type=F32)
        carry_sc[...] = carry

    for p in range(pps):
        kbuf[:, p * PAGE_SIZE:(p + 1) * PAGE_SIZE] = k_refs[p][0].astype(BF16)
        vbuf[:, p * PAGE_SIZE:(p + 1) * PAGE_SIZE] = v_refs[p][0].astype(BF16)
    z = jnp.dot(q_ref[0], kbuf[...], preferred_element_type=F32)
    lk = _neg_softplus(z)
    carry = carry_sc[...]
    parts = [None] * pps
    for p in reversed(range(pps)):
        cols = slice(p * PAGE_SIZE, (p + 1) * PAGE_SIZE)
        parts[p], carry = page_weights(z[:, cols], lk[:, cols], carry)
    a = jnp.concatenate(parts, axis=1)
    acc_sc[...] += lax.dot_general(a.astype(BF16), vbuf[...], (((1,), (1,)), ((), ())),
                                   preferred_element_type=F32)
    carry_sc[...] = carry

    @pl.when(c == n_steps - 1)
    def _():
        o_ref[0] = _pick_heads(acc_sc[...], n_new)


def _sample_attention(page_table, qi_rows, wi_rows, qa_bd, qb_bd, new_pages, caches, top_k):
    db, n_pages = page_table.shape
    n_new = qi_rows.shape[1] // N_HEADS
    rows = N_HEADS * n_new
    pt_flat = page_table.reshape(-1)
    n_pos_bits = int(n_pages * PAGE_SIZE + PAGE_SIZE).bit_length()
    pps = min(PAGES_PER_STEP, n_pages)
    n_steps = n_pages // pps
    assert n_steps * pps == n_pages
    per_b = lambda b, *_: (b, 0, 0)

    def page_spec(width, page_of):
        return pl.BlockSpec((1, width, PAGE_SIZE), lambda b, *a: (a[-1][b * n_pages + page_of(*a[:-1])], 0, 0))

    idx_pages = [page_spec(IDX_DIM, lambda p=p: p) for p in range(n_pages)]
    bias = pl.pallas_call(
        functools.partial(_idx_sample_kernel, n_pages=n_pages, top_k=top_k, n_new=n_new,
                          n_pos_bits=n_pos_bits),
        grid_spec=pltpu.PrefetchScalarGridSpec(
            num_scalar_prefetch=1,
            grid=(db,),
            in_specs=[pl.BlockSpec((1, rows, IDX_DIM), per_b), pl.BlockSpec((1, rows, LANES), per_b)]
                     + idx_pages + [pl.BlockSpec((1, IDX_DIM, PAGE_SIZE), per_b)],
            out_specs=pl.BlockSpec((1, n_new, (n_pages + 1) * LANES), per_b),
            scratch_shapes=[pltpu.VMEM((n_pages + 1, n_new, LANES), I32)],
        ),
        out_shape=jax.ShapeDtypeStruct((db, n_new, (n_pages + 1) * LANES), F32),
        compiler_params=_cparams(("arbitrary",)),
        name="idx_sample",
    )(pt_flat, qi_rows, wi_rows, *([caches["k_idx"]] * n_pages), new_pages["k_idx"])

    fwd = [page_spec(WIDTH, lambda c, p=p: c * pps + p) for p in range(pps)]
    rev = [page_spec(WIDTH, lambda c, p=p: n_pages - pps * (c + 1) + p) for p in range(pps)]
    new_spec = pl.BlockSpec((1, WIDTH, PAGE_SIZE), per_b)
    q_spec = pl.BlockSpec((1, rows, WIDTH), per_b)
    o_spec = pl.BlockSpec((1, n_new, WIDTH), per_b)
    page_bufs = [pltpu.VMEM((WIDTH, pps * PAGE_SIZE), BF16)] * 2
    oa = pl.pallas_call(
        functools.partial(_att_sample_kernel, pps=pps, n_steps=n_steps, n_new=n_new),
        grid_spec=pltpu.PrefetchScalarGridSpec(
            num_scalar_prefetch=1,
            grid=(db, n_steps),
            in_specs=[q_spec,
                      pl.BlockSpec((1, n_new, pps * LANES), lambda b, c, pt: (b, 0, c)),
                      pl.BlockSpec((1, n_new, LANES), lambda b, c, pt: (b, 0, n_pages))]
                     + fwd + fwd + [new_spec, new_spec],
            out_specs=o_spec,
            scratch_shapes=page_bufs + [pltpu.VMEM((rows, 1), F32), pltpu.VMEM((rows, 1), F32),
                                        pltpu.VMEM((rows, WIDTH), F32)],
        ),
        out_shape=jax.ShapeDtypeStruct((db, n_new, WIDTH), F32),
        compiler_params=_cparams(("arbitrary", "arbitrary")),
        name="att_sample",
    )(pt_flat, qa_bd, bias, bias, *([caches["k_a"]] * pps), *([caches["v_a"]] * pps),
      new_pages["k_a"], new_pages["v_a"])

    ob = pl.pallas_call(
        functools.partial(_sb_sample_kernel, pps=pps, n_steps=n_steps, n_new=n_new),
        grid_spec=pltpu.PrefetchScalarGridSpec(
            num_scalar_prefetch=1,
            grid=(db, n_steps),
            in_specs=[q_spec] + rev + rev + [new_spec, new_spec],
            out_specs=o_spec,
            scratch_shapes=page_bufs + [pltpu.VMEM((rows, 1), F32), pltpu.VMEM((rows, WIDTH), F32)],
        ),
        out_shape=jax.ShapeDtypeStruct((db, n_new, WIDTH), F32),
        compiler_params=_cparams(("arbitrary", "arbitrary")),
        name="sb_sample",
    )(pt_flat, qb_bd, *([caches["k_b"]] * pps), *([caches["v_b"]] * pps),
      new_pages["k_b"], new_pages["v_b"])
    return oa, ob


def _rms(x, g):
    ms = jnp.mean(x * x, axis=-1, keepdims=True)
    return (x * lax.rsqrt(ms + RMS_EPS)) * g


def _merge_kernel(x_ref, oa_ref, ob_ref, gate_ref, wa_ref, wb_ref, wo_ref, g_ref, x1_ref, h2_ref):
    pa = jnp.dot(oa_ref[0].astype(BF16), wa_ref[...], preferred_element_type=F32)
    pb = jnp.dot(ob_ref[0].astype(BF16), wb_ref[...], preferred_element_type=F32)
    ga = gate_ref[0, :, 0:D_MODEL]
    gb = gate_ref[0, :, D_MODEL:2 * D_MODEL]
    m = jax.nn.sigmoid(ga) * pa + jax.nn.sigmoid(gb) * pb
    x1 = x_ref[0] + jnp.dot(m.astype(BF16), wo_ref[...], preferred_element_type=F32)
    x1_ref[0] = x1
    h2_ref[0] = _rms(x1, g_ref[...]).astype(BF16)


def _merge(x3, oa, ob, gates, wa, wb, wo, g_ffn):
    groups, rows, _ = x3.shape
    blk = lambda n: pl.BlockSpec((1, TR, n), lambda b, t: (b, t, 0))
    const = lambda shape: _resident(shape, lambda b, t: (0,) * len(shape))
    return pl.pallas_call(
        _merge_kernel,
        grid=(groups, rows // TR),
        in_specs=[blk(D_MODEL), blk(WIDTH), blk(WIDTH), blk(2 * D_MODEL),
                  const((WIDTH, D_MODEL)), const((WIDTH, D_MODEL)), const((D_MODEL, D_MODEL)),
                  const((1, D_MODEL))],
        out_specs=[blk(D_MODEL), blk(D_MODEL)],
        out_shape=[jax.ShapeDtypeStruct((groups, rows, D_MODEL), F32),
                   jax.ShapeDtypeStruct((groups, rows, D_MODEL), BF16)],
        compiler_params=_cparams(("arbitrary", "arbitrary")),
        name="merge",
    )(x3, oa, ob, gates, wa, wb, wo, g_ffn)


def _ffn_kernel(*refs, seq_len, slab, has_state):
    if has_state:
        (x1_ref, h2_ref, wup_ref, cw_ref, cb_ref, wdn_ref, gf_ref, s1_ref, s2_ref,
         y_ref, a_ref, buf) = refs
    else:
        x1_ref, h2_ref, wup_ref, cw_ref, cb_ref, wdn_ref, gf_ref, y_ref, a_ref, buf = refs
    t = pl.program_id(1)
    u = jnp.dot(h2_ref[0], wup_ref[...], preferred_element_type=F32)
    a = u[:, :D_FF]
    b = u[:, D_FF:]

    @pl.when(t == 0)
    def _():
        buf[0:SUBLANES, :] = jnp.zeros((SUBLANES, D_FF), F32)

    buf[SUBLANES:SUBLANES + TR, :] = a
    prev1 = buf[SUBLANES - 1:SUBLANES - 1 + TR, :]
    prev2 = buf[SUBLANES - 2:SUBLANES - 2 + TR, :]
    if has_state:
        tok = lax.broadcasted_iota(I32, (TR, 1), 0) % seq_len
        prev1 = jnp.where(tok == 0, s1_ref[...], prev1)
        prev2 = jnp.where(tok <= 1, s2_ref[...], prev2)
    cw = cw_ref[...]
    a_c = cb_ref[...] + cw[0:1, :] * prev2 + cw[1:2, :] * prev1 + cw[2:3, :] * a
    gelu = 0.5 * a_c * (1.0 + lax.erf(a_c * (2.0 ** -0.5)))
    f = jnp.dot((gelu * b).astype(BF16), wdn_ref[...], preferred_element_type=F32)
    y_ref[0] = _rms(x1_ref[0] + f, gf_ref[...])
    a_ref[0] = a[slab[0]:slab[0] + slab[1], :]
    buf[0:SUBLANES, :] = a[TR - SUBLANES:, :]


def _ffn(x1, h2, w_up, conv_w, conv_b, w_down, g_final, slab, state=None, seq_len=0):
    nseq, rows = x1.shape[0], x1.shape[1]
    nt = rows // TR
    blk = lambda n: pl.BlockSpec((1, TR, n), lambda s, t: (s, t, 0))
    const = lambda shape: _resident(shape, lambda s, t: (0,) * len(shape))
    in_specs = [blk(D_MODEL), blk(D_MODEL), const((D_MODEL, 2 * D_FF)), const((3, D_FF)),
                const((1, D_FF)), const((D_FF, D_MODEL)), const((1, D_MODEL))]
    args = [x1, h2, w_up, conv_w, conv_b, w_down, g_final]
    if state is not None:
        in_specs += [const((TR, D_FF)), const((TR, D_FF))]
        args += list(state)
    return pl.pallas_call(
        functools.partial(_ffn_kernel, seq_len=seq_len, slab=slab, has_state=state is not None),
        grid=(nseq, nt),
        in_specs=in_specs,
        out_specs=[blk(D_MODEL), pl.BlockSpec((1, slab[1], D_FF), lambda s, t: (s, t, 0))],
        out_shape=[jax.ShapeDtypeStruct((nseq, rows, D_MODEL), F32),
                   jax.ShapeDtypeStruct((nseq, nt * slab[1], D_FF), F32)],
        scratch_shapes=[pltpu.VMEM((SUBLANES + TR, D_FF), F32)],
        compiler_params=_cparams(("arbitrary", "arbitrary")),
        name="conv_ffn",
    )(*args)


def _rope_tables(positions):
    half = HEAD_DIM // 2
    inv = ROPE_THETA ** (-np.arange(half, dtype=np.float64) * 2.0 / HEAD_DIM)
    ang = np.asarray(positions, np.float64)[:, None] * inv[None, :]
    cos = np.tile(np.cos(ang), (1, LANES // half))
    sin = np.sin(ang)
    sin = np.tile(np.concatenate([-sin, sin], axis=1), (1, LANES // HEAD_DIM))
    return cos.astype(np.float32), sin.astype(np.float32)


def _permute_w_in(w):
    off = np.cumsum([0, WIDTH, WIDTH, WIDTH, WIDTH, IDX_DIM, N_HEADS, WIDTH, WIDTH, WIDTH,
                     D_MODEL, D_MODEL])
    qa, ka, va, qi, ki, wi, qb, kb, vb, ga, gb = [w[:, off[j]:off[j + 1]] for j in range(11)]
    pad = jnp.zeros((w.shape[0], C_VA - C_WI - N_HEADS), w.dtype)
    out = jnp.concatenate([ga, gb, qa, ka, qi, ki, wi, pad, va, qb, kb, vb], axis=1)
    assert out.shape[1] == NW
    return out.astype(BF16)


def _block_diag_rows(q):
    db, n_new, _ = q.shape
    head_of_col = jnp.arange(WIDTH) // HEAD_DIM
    keep = head_of_col[None, :] == jnp.arange(N_HEADS)[:, None]
    out = jnp.where(keep[None, :, None, :], q[:, None, :, :], 0)
    return out.reshape(db, N_HEADS * n_new, WIDTH)


def _pad_page_t(x):
    return jnp.pad(x.transpose(0, 2, 1), ((0, 0), (0, 0), (0, PAGE_SIZE - x.shape[1])))


def kernel(x_prompt, x_sample, cache_k_a, cache_v_a, cache_k_idx, cache_k_b, cache_v_b, state_conv,
           page_table, meta_tokens, g_attn, w_in, w_branch_a, w_branch_b, w_o, g_ffn, w_up, conv_w,
           conv_b, w_down, g_final):
    batch, seq = x_prompt.shape[0], x_prompt.shape[1]
    db, n_new = x_sample.shape[0], x_sample.shape[1]
    n_pages = page_table.shape[1]
    past_len = n_pages * PAGE_SIZE
    s_len = N_META + seq
    s_pad = -(-s_len // TQ) * TQ
    top_k_p = min(TOP_K_MAX, seq // 4)
    top_k_s = min(TOP_K_MAX, (past_len + n_new) // 4)
    assert db * n_new == TR and cache_k_a.shape[0] == 1
    assert (s_len - 2) // SUBLANES == (s_len - 1) // SUBLANES

    meta = jnp.broadcast_to(meta_tokens[None], (batch, N_META, D_MODEL))
    xp = jnp.concatenate([meta, x_prompt, jnp.zeros((batch, s_pad - s_len, D_MODEL), F32)], axis=1)
    xs = x_sample.reshape(1, TR, D_MODEL)
    w_perm = _permute_w_in(w_in[0])
    cos_p, sin_p = _rope_tables(np.arange(s_pad))
    cos_s, sin_s = _rope_tables(past_len + np.arange(TR) % n_new)

    (ka_p, va_p, ki_p, kb_p, vb_p, qa_pb, qi_pb, qb_pb, ka_pb, va_pb, kb_pb, vb_pb, ki_pb, kiwi_p,
     gate_p) = _in_projection(xp, g_attn, w_perm, jnp.asarray(cos_p), jnp.asarray(sin_p),
                              lambda b, t: t, s_len)
    (ka_s, va_s, ki_s, kb_s, vb_s, qa_sb, qi_sb, qb_sb, ka_sb, va_sb, kb_sb, vb_sb, ki_sb, kiwi_s,
     gate_s) = _in_projection(xs, g_attn, w_perm, jnp.asarray(cos_s), jnp.asarray(sin_s),
                              lambda b, t: t, TR)

    oa_p = _dsa_prompt(qa_pb, qi_pb, kiwi_p, ka_pb, ki_pb, va_pb, top_k_p)
    ob_p = _sb_prompt(qb_pb, kb_pb, vb_pb)

    per_seq = lambda a: a.reshape(db, n_new, a.shape[-1])
    qi_rows = per_seq(qi_sb).reshape(db, n_new, N_HEADS, IDX_DIM).transpose(0, 2, 1, 3)
    qi_rows = qi_rows.reshape(db, N_HEADS * n_new, IDX_DIM)
    wi_s = per_seq(kiwi_s)[..., C_WI - C_KI:C_WI - C_KI + N_HEADS]
    wi_rows = jnp.broadcast_to(wi_s.transpose(0, 2, 1).reshape(db, N_HEADS * n_new, 1),
                               (db, N_HEADS * n_new, LANES))
    n_phys = cache_k_a.shape[1]
    pages_t = lambda c: jnp.moveaxis(c[0], 1, -1).reshape(n_phys, -1, PAGE_SIZE)
    caches = {"k_a": pages_t(cache_k_a), "v_a": pages_t(cache_v_a), "k_idx": pages_t(cache_k_idx),
              "k_b": pages_t(cache_k_b), "v_b": pages_t(cache_v_b)}
    new_pages = {name: _pad_page_t(per_seq(arr)) for name, arr in
                 (("k_a", ka_sb), ("v_a", va_sb), ("k_idx", ki_sb), ("k_b", kb_sb), ("v_b", vb_sb))}
    oa_s, ob_s = _sample_attention(page_table, qi_rows, wi_rows, _block_diag_rows(per_seq(qa_sb)),
                                   _block_diag_rows(per_seq(qb_sb)), new_pages, caches, top_k_s)

    merge_w = (w_branch_a[0].astype(BF16), w_branch_b[0].astype(BF16), w_o[0].astype(BF16), g_ffn)
    x1_p, h2_p = _merge(xp, oa_p, ob_p, gate_p, *merge_w)
    x1_s, h2_s = _merge(xs, oa_s.reshape(1, TR, WIDTH), ob_s.reshape(1, TR, WIDTH), gate_s, *merge_w)
    ffn_w = (w_up[0].astype(BF16), conv_w[0], conv_b, w_down[0].astype(BF16), g_final[None])
    tail_tile, tail_off = (s_len - 2) // TR, (s_len - 2) % TR
    slab_off = tail_off // SUBLANES * SUBLANES
    y_p, a_p = _ffn(x1_p, h2_p, *ffn_w, slab=(slab_off, SUBLANES))
    st = state_conv[0]
    zeros = jnp.zeros((db, n_new - 1, D_FF), F32)
    s1 = jnp.concatenate([st[:, 1:2], zeros], axis=1).reshape(TR, D_FF)
    s2 = jnp.concatenate([st[:, 0:1], st[:, 1:2], zeros[:, 1:]], axis=1).reshape(TR, D_FF)
    y_s, a_s = _ffn(x1_s, h2_s, *ffn_w, slab=(0, TR), state=(s1, s2), seq_len=n_new)

    y_prompt = y_p[:, N_META:s_len]
    y_sample = y_s.reshape(db, n_new, D_MODEL)
    a_tail = tail_tile * SUBLANES + tail_off - slab_off
    conv_p = a_p[:, a_tail:a_tail + 2][None]
    conv_s = a_s.reshape(db, n_new, D_FF)[:, n_new - 2:][None]
    heads_p = lambda a: a.reshape(1, batch, s_len, N_HEADS, HEAD_DIM)
    heads_s = lambda a: a.reshape(1, db, n_new, N_HEADS, HEAD_DIM)
    return (y_prompt, y_sample,
            heads_p(ka_p), heads_p(va_p), ki_p[None], heads_p(kb_p), heads_p(vb_p), conv_p,
            heads_s(ka_s), heads_s(va_s), ki_s.reshape(1, db, n_new, IDX_DIM),
            heads_s(kb_s), heads_s(vb_s), conv_s)
```
